```python
import jax, jax.numpy as jnp
from jax import lax
import numpy as np

D_MODEL = 2048
BATCH = 2
SEQ = 8192
DEPTH = 1
DEC_BATCH = 128
DEC_SEQ = 8
PAST_LEN = 16384
PAGE_SIZE = 128

HEAD_DIM = 64
MIX_WIDTH = D_MODEL
N_HEADS_A = MIX_WIDTH // 2 // HEAD_DIM
N_KV_A = N_HEADS_A // 4
GROUP_A = N_HEADS_A // N_KV_A
N_HEADS_B = MIX_WIDTH // 2 // HEAD_DIM
WIN_A = 128
DIL_BRANCHES = ((128, 1), (512, 4), (2048, 16))
WIN_B = max(w for w, _ in DIL_BRANCHES)
BLK = 128
D_FF = 4 * D_MODEL
ROPE_THETA = 10000.0
LN_EPS = 1e-5
DN_ALPHA = (2.0 * DEPTH) ** 0.25
DN_BETA = (8.0 * DEPTH) ** -0.25
SCALE = HEAD_DIM ** -0.5
NEG_INF = -1e30
QA = N_HEADS_A * HEAD_DIM
KA = N_KV_A * HEAD_DIM
VA = KA
QB = N_HEADS_B * HEAD_DIM
KB = QB
VB = QB
IN_WIDTH = QA + KA + VA + QB + KB + VB

kernel_name = "hymba_swa_sink_dilated_deepnorm"


def layer_norm(x, g, b):
    xf = x.astype(jnp.float32)
    mu = jnp.mean(xf, -1, keepdims=True)
    var = jnp.mean(jnp.square(xf - mu), -1, keepdims=True)
    return ((xf - mu) * lax.rsqrt(var + LN_EPS) * g.astype(jnp.float32) + b.astype(jnp.float32)).astype(x.dtype)


def rope(x, pos):
    half = HEAD_DIM // 2
    inv = 1.0 / (ROPE_THETA ** (jnp.arange(half, dtype=jnp.float32) / half))
    ang = pos.astype(jnp.float32)[:, None] * inv[None, :]
    cos = jnp.cos(ang)[:, None, :]
    sin = jnp.sin(ang)[:, None, :]
    xf = x.astype(jnp.float32)
    x1, x2 = xf[..., :half], xf[..., half:]
    return jnp.concatenate([x1 * cos - x2 * sin, x2 * cos + x1 * sin], -1).astype(x.dtype)


def project(x, w_in, pos):
    B, T = x.shape[:2]
    h = jnp.einsum('btd,de->bte', x, w_in)
    cuts = [int(c) for c in np.cumsum([QA, KA, VA, QB, KB])]
    qa, ka, va, qb, kb, vb = jnp.split(h, cuts, axis=-1)
    qa = rope(qa.reshape(B, T, N_HEADS_A, HEAD_DIM), pos)
    ka = rope(ka.reshape(B, T, N_KV_A, HEAD_DIM), pos)
    va = va.reshape(B, T, N_KV_A, HEAD_DIM)
    qb = rope(qb.reshape(B, T, N_HEADS_B, HEAD_DIM), pos)
    kb = rope(kb.reshape(B, T, N_HEADS_B, HEAD_DIM), pos)
    vb = vb.reshape(B, T, N_HEADS_B, HEAD_DIM)
    return qa, ka, va, qb, kb, vb


def band_keys(x):
    X, L = x.shape[:2]
    nb = L // BLK
    xb = x.reshape((X, nb, BLK) + x.shape[2:])
    prev = jnp.concatenate([jnp.zeros_like(xb[:, :1]), xb[:, :-1]], axis=1)
    return jnp.concatenate([prev, xb], axis=2)


def band_pos(nb):
    return (jnp.arange(nb)[:, None] - 1) * BLK + jnp.arange(2 * BLK)[None, :]


def sink_attention(q, k, v, q_pos, k_pos, sinks):
    s = jnp.einsum('bntkgd,bnskd->bnkgts', q, k).astype(jnp.float32) * SCALE
    dist = q_pos[:, :, None] - k_pos[:, None, :]
    valid = (dist >= 0) & (dist < WIN_A) & (k_pos[:, None, :] >= 0)
    s = jnp.where(valid[None, :, None, None], s, NEG_INF)
    sink = sinks.astype(jnp.float32).reshape(N_KV_A, GROUP_A)[None, None, :, :, None, None]
    m = jnp.maximum(jnp.max(s, -1, keepdims=True), sink)
    p = jnp.exp(s - m)
    denom = jnp.sum(p, -1, keepdims=True) + jnp.exp(sink - m)
    return jnp.einsum('bnkgts,bnskd->bntkgd', (p / denom).astype(v.dtype), v)


def window_sink_prompt(qa, ka, va, sinks):
    B, L = qa.shape[:2]
    nb = L // BLK
    q = qa.reshape(B, nb, BLK, N_KV_A, GROUP_A, HEAD_DIM)
    o = sink_attention(q, band_keys(ka), band_keys(va), jnp.arange(L).reshape(nb, BLK), band_pos(nb), sinks)
    return o.reshape(B, L, QA)


def window_sink_sample(qa, ka, va, cache_k, cache_v, sinks):
    DB, T = qa.shape[:2]
    n_past = cache_k.shape[1]
    k_ext = jnp.concatenate([cache_k, ka], axis=1)
    v_ext = jnp.concatenate([cache_v, va], axis=1)
    q = qa.reshape(DB, 1, T, N_KV_A, GROUP_A, HEAD_DIM)
    q_pos = (PAST_LEN + jnp.arange(T))[None]
    k_pos = (PAST_LEN - n_past + jnp.arange(n_past + T))[None]
    o = sink_attention(q, k_ext[:, None], v_ext[:, None], q_pos, k_pos, sinks)
    return o.reshape(DB, T, QA), k_ext[:, -n_past:], v_ext[:, -n_past:]


def probs_lse(s, valid):
    s = jnp.where(valid, s, NEG_INF)
    m = jnp.max(s, -1, keepdims=True)
    e = jnp.exp(s - m)
    l = jnp.sum(e, -1, keepdims=True)
    return e / l, (m + jnp.log(l))[..., 0]


def dilated_branch_prompt(q, k, v, window, r):
    B, L, H, D = q.shape
    n_keys = window // r
    M = L // r
    Mp = -(-M // BLK) * BLK
    nb = Mp // BLK

    def sub(x):
        x = x.reshape(B, M, r, H, D).transpose(0, 2, 1, 3, 4)
        x = jnp.pad(x, ((0, 0), (0, 0), (0, Mp - M), (0, 0), (0, 0)))
        return x.reshape(B * r, Mp, H, D)

    qb = sub(q).reshape(B * r, nb, BLK, H, D)
    kb = band_keys(sub(k))
    vb = band_keys(sub(v))
    qp = jnp.arange(Mp).reshape(nb, BLK)
    kp = band_pos(nb)
    dist = qp[:, :, None] - kp[:, None, :]
    valid = (dist >= 0) & (dist <= n_keys) & (kp[:, None, :] >= 0)
    s = jnp.einsum('xnthd,xnshd->xnhts', qb, kb).astype(jnp.float32) * SCALE
    p, lse = probs_lse(s, valid[None, :, None])
    o = jnp.einsum('xnhts,xnshd->xnthd', p.astype(vb.dtype), vb)
    o = o.reshape(B, r, Mp, H, D)[:, :, :M].transpose(0, 2, 1, 3, 4).reshape(B, L, H, D)
    lse = lse.transpose(0, 1, 3, 2).reshape(B, r, Mp, H)[:, :, :M].transpose(0, 2, 1, 3).reshape(B, L, H)
    return o, lse


def dilated_branch_sample(q, k_ext, v_ext, n_past, window, r):
    T = q.shape[1]
    n_keys = window // r
    idx = n_past + jnp.arange(T)[:, None] - r * jnp.arange(n_keys + 1)[None, :]
    valid = idx >= 0
    idx = jnp.maximum(idx, 0)
    kg = k_ext[:, idx]
    vg = v_ext[:, idx]
    s = jnp.einsum('bthd,btjhd->bthj', q, kg).astype(jnp.float32) * SCALE
    p, lse = probs_lse(s, valid[None, :, None, :])
    o = jnp.einsum('bthj,btjhd->bthd', p.astype(vg.dtype), vg)
    return o, lse


def merge_branches(outs, lses):
    w = jax.nn.softmax(jnp.stack(lses, 0), axis=0)
    o = jnp.einsum('ibth,ibthd->bthd', w, jnp.stack(outs, 0).astype(jnp.float32))
    return o.astype(outs[0].dtype)


def dilated_prompt(qb, kb, vb):
    outs, lses = [], []
    for window, r in DIL_BRANCHES:
        o, l = dilated_branch_prompt(qb, kb, vb, window, r)
        outs.append(o)
        lses.append(l)
    B, L = qb.shape[:2]
    return merge_branches(outs, lses).reshape(B, L, QB)


def dilated_sample(qb, kb, vb, cache_k, cache_v):
    DB, T = qb.shape[:2]
    n_past = cache_k.shape[1]
    k_ext = jnp.concatenate([cache_k, kb], axis=1)
    v_ext = jnp.concatenate([cache_v, vb], axis=1)
    outs, lses = [], []
    for window, r in DIL_BRANCHES:
        o, l = dilated_branch_sample(qb, k_ext, v_ext, n_past, window, r)
        outs.append(o)
        lses.append(l)
    return merge_branches(outs, lses).reshape(DB, T, QB), k_ext[:, -n_past:], v_ext[:, -n_past:]


def post_block(x, a, b, w_out, ln1_g, ln1_b, w_up, w_down, ln2_g, ln2_b):
    mix = jnp.concatenate([a, b], axis=-1)
    h = layer_norm(DN_ALPHA * x + jnp.einsum('btm,md->btd', mix, w_out), ln1_g, ln1_b)
    f = jnp.einsum('btf,fd->btd', jnp.square(jax.nn.relu(jnp.einsum('btd,df->btf', h, w_up))), w_down)
    return layer_norm(DN_ALPHA * h + f, ln2_g, ln2_b)


def setup_inputs(seed: int = 0) -> dict:
    key = jax.random.key(seed)
    ks = jax.random.split(key, 16)
    L_A = min(WIN_A, PAST_LEN)
    L_B = min(WIN_B, PAST_LEN)
    f32 = jnp.float32
    col_scale = jnp.concatenate([
        jnp.ones((QA + KA,), f32), jnp.full((VA,), DN_BETA, f32),
        jnp.ones((QB + KB,), f32), jnp.full((VB,), DN_BETA, f32)])
    return {
        "x_prompt": jax.random.normal(ks[0], (BATCH, SEQ, D_MODEL), f32),
        "x_sample": jax.random.normal(ks[1], (DEC_BATCH, DEC_SEQ, D_MODEL), f32),
        "cache_a_k": jax.random.normal(ks[2], (DEPTH, DEC_BATCH, L_A, N_KV_A, HEAD_DIM), f32),
        "cache_a_v": jax.random.normal(ks[3], (DEPTH, DEC_BATCH, L_A, N_KV_A, HEAD_DIM), f32) * DN_BETA,
        "cache_b_k": jax.random.normal(ks[4], (DEPTH, DEC_BATCH, L_B, N_HEADS_B, HEAD_DIM), f32),
        "cache_b_v": jax.random.normal(ks[5], (DEPTH, DEC_BATCH, L_B, N_HEADS_B, HEAD_DIM), f32) * DN_BETA,
        "w_in": jax.random.normal(ks[6], (DEPTH, D_MODEL, IN_WIDTH), f32) * (D_MODEL ** -0.5) * col_scale,
        "sinks": jax.random.normal(ks[7], (DEPTH, N_HEADS_A), f32),
        "w_out": jax.random.normal(ks[8], (DEPTH, MIX_WIDTH, D_MODEL), f32) * (MIX_WIDTH ** -0.5) * DN_BETA,
        "ln1_g": 1.0 + 0.02 * jax.random.normal(ks[9], (DEPTH, D_MODEL), f32),
        "ln1_b": 0.02 * jax.random.normal(ks[10], (DEPTH, D_MODEL), f32),
        "w_up": jax.random.normal(ks[11], (DEPTH, D_MODEL, D_FF), f32) * (D_MODEL ** -0.5),
        "w_down": jax.random.normal(ks[12], (DEPTH, D_FF, D_MODEL), f32) * (D_FF ** -0.5) * DN_BETA,
        "ln2_g": 1.0 + 0.02 * jax.random.normal(ks[13], (DEPTH, D_MODEL), f32),
        "ln2_b": 0.02 * jax.random.normal(ks[14], (DEPTH, D_MODEL), f32),
    }


def reference(x_prompt, x_sample, cache_a_k, cache_a_v, cache_b_k, cache_b_v,
              w_in, sinks, w_out, ln1_g, ln1_b, w_up, w_down, ln2_g, ln2_b):
    L = x_prompt.shape[1]
    T = x_sample.shape[1]
    pos_p = jnp.arange(L)
    pos_s = PAST_LEN + jnp.arange(T)
    hp, hs = x_prompt, x_sample
    pak, pav, pbk, pbv, sak, sav, sbk, sbv = [], [], [], [], [], [], [], []
    for l in range(DEPTH):
        qa, ka, va, qb, kb, vb = project(hp, w_in[l], pos_p)
        a = window_sink_prompt(qa, ka, va, sinks[l])
        b = dilated_prompt(qb, kb, vb)
        hp = post_block(hp, a, b, w_out[l], ln1_g[l], ln1_b[l], w_up[l], w_down[l], ln2_g[l], ln2_b[l])
        pak.append(ka[:, -min(WIN_A, L):])
        pav.append(va[:, -min(WIN_A, L):])
        pbk.append(kb[:, -min(WIN_B, L):])
        pbv.append(vb[:, -min(WIN_B, L):])
        qa, ka, va, qb, kb, vb = project(hs, w_in[l], pos_s)
        a, nak, nav = window_sink_sample(qa, ka, va, cache_a_k[l], cache_a_v[l], sinks[l])
        b, nbk, nbv = dilated_sample(qb, kb, vb, cache_b_k[l], cache_b_v[l])
        hs = post_block(hs, a, b, w_out[l], ln1_g[l], ln1_b[l], w_up[l], w_down[l], ln2_g[l], ln2_b[l])
        sak.append(nak)
        sav.append(nav)
        sbk.append(nbk)
        sbv.append(nbv)
    return (hp, hs, jnp.stack(pak), jnp.stack(pav), jnp.stack(pbk), jnp.stack(pbv),
            jnp.stack(sak), jnp.stack(sav), jnp.stack(sbk), jnp.stack(sbv))
```

```python
import functools

import numpy as np
import jax
import jax.numpy as jnp
from jax import lax
from jax.experimental import pallas as pl
from jax.experimental.pallas import tpu as pltpu

F32 = jnp.float32
BF16 = jnp.bfloat16

HEAD_DIM = 64
LANES = 128
N_HEADS_A = 16
N_KV_A = 4
GROUP_A = N_HEADS_A // N_KV_A
N_HEADS_B = 16
WIN_A = 128
DIL_BRANCHES = ((128, 1), (512, 4), (2048, 16))
WIN_B = 2048
BLK = 128
ROPE_THETA = 10000.0
LN_EPS = 1e-5
DN_ALPHA = 2.0 ** 0.25
SCALE = HEAD_DIM ** -0.5
NEG_INF = -1e30
PAST_LEN = 16384
QA = N_HEADS_A * HEAD_DIM
KA = N_KV_A * HEAD_DIM
QB = N_HEADS_B * HEAD_DIM
IN_WIDTH = QA + 2 * KA + 3 * QB
VMEM_LIMIT = 56 * 1024 * 1024


def _cparams(sem):
    return pltpu.CompilerParams(dimension_semantics=sem, vmem_limit_bytes=VMEM_LIMIT)


def _resident(shape):
    nd = len(shape)
    return pl.BlockSpec(shape, lambda *_: (0,) * nd, pipeline_mode=pl.Buffered(1))


def _qa_head_perm():
    cols = []
    for p in range(N_KV_A // 2):
        for j in range(GROUP_A):
            for half in range(2):
                head = GROUP_A * (2 * p + half) + j
                cols.extend(range(head * HEAD_DIM, (head + 1) * HEAD_DIM))
    return np.asarray(cols, np.int32)


def _rope_tables(pos):
    half = HEAD_DIM // 2
    inv = 1.0 / (ROPE_THETA ** (jnp.arange(half, dtype=F32) / half))
    ang = pos.astype(F32)[:, None] * inv[None, :]
    cos = jnp.cos(ang)
    sin = jnp.sin(ang)
    return (jnp.concatenate([cos, cos, cos, cos], -1),
            jnp.concatenate([-sin, sin, -sin, sin], -1))


def _proj_kernel(x_ref, w_ref, cos_ref, sin_ref, *out_refs, tm, cache_rows_a, first_cache_tile, last_tile):
    qa_ref, ka_ref, va_ref, qb_ref, kb_ref, vb_ref = out_refs[:6]
    tails = out_refs[6:]
    t = pl.program_id(1)
    xb = x_ref[0].astype(BF16)
    cos = cos_ref[...]
    sin = sin_ref[...]
    lane = lax.broadcasted_iota(jnp.int32, (tm, LANES), 1)
    lo_half = (lane % HEAD_DIM) < (HEAD_DIM // 2)

    def rope(hg):
        swapped = jnp.where(lo_half, pltpu.roll(hg, LANES - HEAD_DIM // 2, 1), pltpu.roll(hg, HEAD_DIM // 2, 1))
        return hg * cos + swapped * sin

    def groups(col0, n):
        h = jnp.dot(xb, w_ref[:, col0:col0 + n * LANES], preferred_element_type=F32)
        return [h[:, g * LANES:(g + 1) * LANES] for g in range(n)]

    def put(ref, g, val):
        ref[0, :, g * LANES:(g + 1) * LANES] = val.astype(ref.dtype)

    for c in range(QA // 512):
        for g, hg in enumerate(groups(c * 512, 4)):
            put(qa_ref, c * 4 + g, rope(hg) * SCALE)

    kva = groups(QA, 4)
    ka_g = [rope(kva[0]), rope(kva[1])]
    va_g = [kva[2], kva[3]]
    for g in range(2):
        put(ka_ref, g, ka_g[g])
        put(va_ref, g, va_g[g])
    if tails:
        @pl.when(t == last_tile)
        def _():
            for g in range(2):
                tails[0][0, :, g * LANES:(g + 1) * LANES] = ka_g[g][tm - cache_rows_a:, :]
                tails[1][0, :, g * LANES:(g + 1) * LANES] = va_g[g][tm - cache_rows_a:, :]

    base = QA + 2 * KA
    for c in range(QB // 512):
        for g, hg in enumerate(groups(base + c * 512, 4)):
            put(qb_ref, c * 4 + g, rope(hg) * SCALE)

    for which, (ref, roped) in enumerate(((kb_ref, True), (vb_ref, False))):
        for c in range(QB // 512):
            vals = groups(base + QB * (1 + which) + c * 512, 4)
            if roped:
                vals = [rope(v) for v in vals]
            for g, v in enumerate(vals):
                put(ref, c * 4 + g, v)
            if tails:
                @pl.when(t >= first_cache_tile)
                def _(vals=vals, c=c, which=which):
                    for g, v in enumerate(vals):
                        tails[2 + which][0, :, (c * 4 + g) * LANES:(c * 4 + g + 1) * LANES] = v


def _project(x, w_in_bf, cos, sin, *, tm, out_dtype, with_tails):
    B, L, D = x.shape
    nt = L // tm
    assert L % tm == 0
    widths = (QA, KA, KA, QB, QB, QB)
    dtypes = (out_dtype,) * 6
    out_shape = [jax.ShapeDtypeStruct((B, L, w), dt) for w, dt in zip(widths, dtypes)]
    out_specs = [pl.BlockSpec((1, tm, w), lambda b, t: (b, t, 0)) for w in widths]
    rows_a = min(WIN_A, L)
    rows_b = min(WIN_B, L)
    first_cache_tile = (L - rows_b) // tm
    if with_tails:
        assert rows_a <= tm and (L - rows_b) % tm == 0
        out_shape += [jax.ShapeDtypeStruct((B, rows_a, KA), F32)] * 2
        out_shape += [jax.ShapeDtypeStruct((B, rows_b, QB), F32)] * 2
        out_specs += [pl.BlockSpec((1, rows_a, KA), lambda b, t: (b, 0, 0))] * 2
        out_specs += [pl.BlockSpec((1, tm, QB), lambda b, t: (b, jnp.maximum(t - first_cache_tile, 0), 0))] * 2
    kern = functools.partial(_proj_kernel, tm=tm, cache_rows_a=rows_a, first_cache_tile=first_cache_tile,
                             last_tile=nt - 1)
    return pl.pallas_call(
        kern,
        grid=(B, nt),
        in_specs=[pl.BlockSpec((1, tm, D), lambda b, t: (b, t, 0)),
                  _resident((D, IN_WIDTH)),
                  pl.BlockSpec((tm, LANES), lambda b, t: (t, 0)),
                  pl.BlockSpec((tm, LANES), lambda b, t: (t, 0))],
        out_specs=out_specs,
        out_shape=out_shape,
        compiler_params=_cparams(("arbitrary", "arbitrary")),
        name="proj_tails" if with_tails else "proj",
    )(x, w_in_bf, cos, sin)


def _band_attn_kernel(*refs, n_kgroups, q_per_k, inclusive, has_sink, has_prev, emit_state, block_axis):
    it = iter(refs)
    q_ref, kp_ref, kc_ref, vp_ref, vc_ref = (next(it) for _ in range(5))
    sink_ref = next(it) if has_sink else None
    oprev_ref, lprev_ref = (next(it), next(it)) if has_prev else (None, None)
    o_ref = next(it)
    lse_ref = next(it) if emit_state else None

    mt = pl.program_id(block_axis)
    row = lax.broadcasted_iota(jnp.int32, (2 * BLK, 2 * BLK), 0)
    col = lax.broadcasted_iota(jnp.int32, (2 * BLK, 2 * BLK), 1)
    dist = (row % BLK) + BLK - col
    valid = (dist >= 0) & ((dist <= BLK) if inclusive else (dist < BLK)) & ((col >= BLK) | (mt > 0))
    lane = lax.broadcasted_iota(jnp.int32, (BLK, LANES), 1)
    lane_lo = lane < HEAD_DIM
    upper_rows = lax.broadcasted_iota(jnp.int32, (2 * BLK, 1), 0) >= BLK
    ones = jnp.ones((2 * BLK, LANES), BF16)
    zero_q = jnp.zeros((BLK, LANES), BF16)

    for n in range(n_kgroups):
        ksl = slice(n * LANES, (n + 1) * LANES)
        k2 = jnp.concatenate([kp_ref[0, :, ksl], kc_ref[0, :, ksl]], axis=0)
        v2 = jnp.concatenate([vp_ref[0, :, ksl], vc_ref[0, :, ksl]], axis=0)
        v2e = jnp.concatenate([v2, ones], axis=1)
        for j in range(q_per_k):
            g = n * q_per_k + j
            gsl = slice(g * LANES, (g + 1) * LANES)
            qg = q_ref[0, :, gsl]
            qs = jnp.concatenate([jnp.where(lane_lo, qg, zero_q), jnp.where(lane_lo, zero_q, qg)], axis=0)
            s = lax.dot_general(qs, k2, (((1,), (1,)), ((), ())), preferred_element_type=F32)
            s = jnp.where(valid, s, NEG_INF)
            m = jnp.max(s, axis=1, keepdims=True)
            if has_sink:
                lo_head = 2 * GROUP_A * n + j
                sink = jnp.where(upper_rows, sink_ref[lo_head + GROUP_A], sink_ref[lo_head])
                m = jnp.maximum(m, sink)
            p = jnp.exp(s - m)
            r = jnp.dot(p.astype(BF16), v2e, preferred_element_type=F32)
            l = r[:, LANES:]
            if has_sink:
                l = l + jnp.exp(sink - m)
            on = r[:, :LANES] / l
            o = jnp.where(lane_lo, on[:BLK], on[BLK:])
            if has_prev or emit_state:
                lse_rows = m + jnp.log(l)
                lse = jnp.where(lane_lo, lse_rows[:BLK], lse_rows[BLK:])
            if has_prev:
                lse_p = lprev_ref[0, :, gsl]
                mx = jnp.maximum(lse_p, lse)
                w_p = jnp.exp(lse_p - mx)
                w_c = jnp.exp(lse - mx)
                den = w_p + w_c
                o = (oprev_ref[0, :, gsl] * w_p + o * w_c) / den
                lse = mx + jnp.log(den)
            o_ref[0, :, gsl] = o.astype(o_ref.dtype)
            if emit_state:
                lse_ref[0, :, gsl] = lse


def _band_attention(q, k, v, *, stride, inclusive, sinks=None, prev=None, emit_state=False, out_dtype=BF16,
                    name="band_attn"):
    B, L, Wq = q.shape
    Wk = k.shape[-1]
    M = L // stride
    assert L % stride == 0 and M % BLK == 0
    nb = M // BLK

    def view(x):
        return x.reshape(B, M, stride * x.shape[-1])

    cur = lambda w: pl.BlockSpec((1, BLK, w), lambda b, r, m: (b, m, r))
    prv = lambda w: pl.BlockSpec((1, BLK, w), lambda b, r, m: (b, jnp.maximum(m - 1, 0), r))
    args = [view(q), view(k), view(k), view(v), view(v)]
    in_specs = [cur(Wq), prv(Wk), cur(Wk), prv(Wk), cur(Wk)]
    if sinks is not None:
        args.append(sinks)
        in_specs.append(pl.BlockSpec(memory_space=pltpu.SMEM))
    if prev is not None:
        args += [view(prev[0]), view(prev[1])]
        in_specs += [cur(Wq), cur(Wq)]
    out_shape = [jax.ShapeDtypeStruct((B, M, stride * Wq), out_dtype)]
    out_specs = [cur(Wq)]
    if emit_state:
        out_shape.append(jax.ShapeDtypeStruct((B, M, stride * Wq), F32))
        out_specs.append(cur(Wq))
    kern = functools.partial(
        _band_attn_kernel, n_kgroups=Wk // LANES, q_per_k=Wq // Wk, inclusive=inclusive,
        has_sink=sinks is not None, has_prev=prev is not None, emit_state=emit_state, block_axis=2)
    outs = pl.pallas_call(
        kern,
        grid=(B, stride, nb),
        in_specs=in_specs,
        out_specs=out_specs,
        out_shape=out_shape,
        compiler_params=_cparams(("arbitrary",) * 3),
        name=name,
    )(*args)
    outs = [o.reshape(B, L, Wq) for o in outs]
    return outs if emit_state else outs[0]


def _sample_masks(T, n_past_a, n_past_b):
    t = np.arange(T)[:, None]
    dist_c = n_past_a + t - np.arange(n_past_a)[None, :]
    dist_n = t - np.arange(LANES)[None, :]
    a_c = ((dist_c >= 0) & (dist_c < WIN_A)).astype(np.float32)
    a_n = ((dist_n >= 0) & (dist_n < WIN_A)).astype(np.float32)
    dist_c = n_past_b + t - np.arange(n_past_b)[None, :]
    b_c = np.zeros((T, n_past_b), np.float32)
    b_n = np.zeros((T, LANES), np.float32)
    for window, r in DIL_BRANCHES:
        b_c += ((dist_c >= 0) & (dist_c % r == 0) & (dist_c // r <= window // r)).astype(np.float32)
        b_n += ((dist_n >= 0) & (dist_n % r == 0) & (dist_n // r <= window // r)).astype(np.float32)
    tile = lambda m, reps: np.tile(m, (reps, 1))
    return (tile(a_c, 2 * GROUP_A), tile(a_n, 2 * GROUP_A), tile(b_c, N_HEADS_B), tile(b_n, N_HEADS_B))


def _sample_kernel(qa_ref, kan_ref, van_ref, qb_ref, kbn_ref, vbn_ref, cak_ref, cav_ref, cbk_ref, cbv_ref,
                   sink_ref, mac_ref, man_ref, mbc_ref, mbn_ref,
                   a_ref, b_ref, oak_ref, oav_ref, obk_ref, obv_ref, sems, *, T, chunk):
    b = pl.program_id(0)
    n_past_a = cak_ref.shape[1]
    n_past_b = cbk_ref.shape[1]

    copies = [
        pltpu.make_async_copy(cbk_ref.at[0, pl.ds(T, n_past_b - T)], obk_ref.at[b, pl.ds(0, n_past_b - T)], sems.at[0]),
        pltpu.make_async_copy(cbv_ref.at[0, pl.ds(T, n_past_b - T)], obv_ref.at[b, pl.ds(0, n_past_b - T)], sems.at[1]),
        pltpu.make_async_copy(kbn_ref.at[0], obk_ref.at[b, pl.ds(n_past_b - T, T)], sems.at[2]),
        pltpu.make_async_copy(vbn_ref.at[0], obv_ref.at[b, pl.ds(n_past_b - T, T)], sems.at[3]),
    ]
    for c in copies:
        c.start()

    oak_ref[0, :n_past_a - T, :] = cak_ref[0, T:, :]
    oak_ref[0, n_past_a - T:, :] = kan_ref[0]
    oav_ref[0, :n_past_a - T, :] = cav_ref[0, T:, :]
    oav_ref[0, n_past_a - T:, :] = van_ref[0]

    nt_dims = (((1,), (1,)), ((), ()))

    def pad_rows(x):
        return jnp.concatenate([x, jnp.zeros((LANES - T, x.shape[1]), F32)], axis=0).astype(BF16)

    def softmax_parts(s_c, s_n, m_c, m_n, sink=None):
        s_c = jnp.where(m_c > 0, s_c, NEG_INF)
        s_n = jnp.where(m_n > 0, s_n, NEG_INF)
        m = jnp.maximum(jnp.max(s_c, axis=1, keepdims=True), jnp.max(s_n, axis=1, keepdims=True))
        if sink is not None:
            m = jnp.maximum(m, sink)
        p_c = m_c * jnp.exp(s_c - m)
        p_n = m_n * jnp.exp(s_n - m)
        l = jnp.sum(p_c, axis=1, keepdims=True) + jnp.sum(p_n, axis=1, keepdims=True)
        if sink is not None:
            l = l + jnp.exp(sink - m)
        return p_c, p_n, l

    rows_a = 2 * GROUP_A * T
    lane_a = lax.broadcasted_iota(jnp.int32, (T, LANES), 1)
    lo_a = lane_a < HEAD_DIM
    for p in range(N_KV_A // 2):
        ksl = slice(p * LANES, (p + 1) * LANES)
        blocks = []
        for j in range(GROUP_A):
            qg = qa_ref[0, :, (p * GROUP_A + j) * LANES:(p * GROUP_A + j + 1) * LANES]
            blocks += [jnp.where(lo_a, qg, 0.0), jnp.where(lo_a, 0.0, qg)]
        qs = jnp.concatenate(blocks, axis=0).astype(BF16)
        kc = cak_ref[0, :, ksl].astype(BF16)
        kn = pad_rows(kan_ref[0, :, ksl])
        s_c = lax.dot_general(qs, kc, nt_dims, preferred_element_type=F32)
        s_n = lax.dot_general(qs, kn, nt_dims, preferred_element_type=F32)
        p_c, p_n, l = softmax_parts(s_c, s_n, mac_ref[...], man_ref[...], sink_ref[p][:, 0:1])
        r = jnp.dot(p_c.astype(BF16), cav_ref[0, :, ksl].astype(BF16), preferred_element_type=F32)
        r = r + jnp.dot(p_n.astype(BF16), pad_rows(van_ref[0, :, ksl]), preferred_element_type=F32)
        r = r / l
        for j in range(GROUP_A):
            lo_rows = r[(2 * j) * T:(2 * j + 1) * T]
            hi_rows = r[(2 * j + 1) * T:(2 * j + 2) * T]
            g = p * GROUP_A + j
            a_ref[0, :, g * LANES:(g + 1) * LANES] = jnp.where(lo_a, lo_rows, hi_rows).astype(a_ref.dtype)

    rows_b = N_HEADS_B * T
    row_head = lax.broadcasted_iota(jnp.int32, (rows_b, QB), 0) // T
    lane_head = lax.broadcasted_iota(jnp.int32, (rows_b, QB), 1) // HEAD_DIM
    diag = row_head == lane_head
    qblk = jnp.where(diag, jnp.concatenate([qb_ref[0]] * N_HEADS_B, axis=0), 0.0).astype(BF16)
    s_parts = []
    for c in range(n_past_b // chunk):
        kc = cbk_ref[0, c * chunk:(c + 1) * chunk, :].astype(BF16)
        s_parts.append(lax.dot_general(qblk, kc, nt_dims, preferred_element_type=F32))
    s_c = jnp.concatenate(s_parts, axis=1)
    s_n = lax.dot_general(qblk, pad_rows(kbn_ref[0]), nt_dims, preferred_element_type=F32)
    p_c, p_n, l = softmax_parts(s_c, s_n, mbc_ref[...], mbn_ref[...])
    p_cb = p_c.astype(BF16)
    r = jnp.dot(p_n.astype(BF16), pad_rows(vbn_ref[0]), preferred_element_type=F32)
    for c in range(n_past_b // chunk):
        vc = cbv_ref[0, c * chunk:(c + 1) * chunk, :].astype(BF16)
        r = r + jnp.dot(p_cb[:, c * chunk:(c + 1) * chunk], vc, preferred_element_type=F32)
    r = jnp.where(diag, r / l, 0.0)
    out = r[0:T]
    for h in range(1, N_HEADS_B):
        out = out + r[h * T:(h + 1) * T]
    b_ref[0] = out.astype(b_ref.dtype)

    for c in copies:
        c.wait()


def _sample_attention(qa, kan, van, qb, kbn, vbn, cak, cav, cbk, cbv, sinks):
    DB, T, _ = qa.shape
    n_past_a, n_past_b = cak.shape[1], cbk.shape[1]
    assert T % 8 == 0 and T <= n_past_a and T <= n_past_b and T <= LANES
    chunk = min(512, n_past_b)
    assert n_past_b % chunk == 0
    masks = [jnp.asarray(m) for m in _sample_masks(T, n_past_a, n_past_b)]
    heads = np.asarray([[GROUP_A * (2 * p + half) + j for j in range(GROUP_A) for half in range(2) for _ in range(T)]
                        for p in range(N_KV_A // 2)], np.int32)
    sink_rows = jnp.broadcast_to(sinks[heads][:, :, None], heads.shape + (LANES,)).astype(F32)

    per_b = lambda rows, w: pl.BlockSpec((1, rows, w), lambda b: (b, 0, 0))
    in_specs = [per_b(T, QA), per_b(T, KA), per_b(T, KA), per_b(T, QB), per_b(T, QB), per_b(T, QB),
                per_b(n_past_a, KA), per_b(n_past_a, KA), per_b(n_past_b, QB), per_b(n_past_b, QB),
                _resident(sink_rows.shape)] + [_resident(m.shape) for m in masks]
    out_shape = [jax.ShapeDtypeStruct((DB, T, QA), F32), jax.ShapeDtypeStruct((DB, T, QB), F32),
                 jax.ShapeDtypeStruct(cak.shape, F32), jax.ShapeDtypeStruct(cav.shape, F32),
                 jax.ShapeDtypeStruct(cbk.shape, F32), jax.ShapeDtypeStruct(cbv.shape, F32)]
    out_specs = [per_b(T, QA), per_b(T, QB), per_b(n_past_a, KA), per_b(n_past_a, KA),
                 pl.BlockSpec(memory_space=pl.ANY), pl.BlockSpec(memory_space=pl.ANY)]
    return pl.pallas_call(
        functools.partial(_sample_kernel, T=T, chunk=chunk),
        grid=(DB,),
        in_specs=in_specs,
        out_specs=out_specs,
        out_shape=out_shape,
        scratch_shapes=[pltpu.SemaphoreType.DMA((4,))],
        compiler_params=_cparams(("arbitrary",)),
        name="sample_attn",
    )(qa, kan, van, qb, kbn, vbn, cak, cav, cbk, cbv, sink_rows, *masks)


def _layer_norm(z, g, b):
    mu = jnp.mean(z, axis=-1, keepdims=True)
    zc = z - mu
    var = jnp.mean(zc * zc, axis=-1, keepdims=True)
    return zc * lax.rsqrt(var + LN_EPS) * g + b


def _outproj_kernel(x_ref, a_ref, b_ref, wa_ref, wb_ref, g_ref, be_ref, h_ref):
    proj = jnp.dot(a_ref[...].astype(BF16), wa_ref[...], preferred_element_type=F32)
    proj = proj + jnp.dot(b_ref[...].astype(BF16), wb_ref[...], preferred_element_type=F32)
    h_ref[...] = _layer_norm(DN_ALPHA * x_ref[...] + proj, g_ref[...], be_ref[...])


def _outproj_ln(x, a, b, w_a, w_b, g, be, *, tm):
    N, D = x.shape
    assert N % tm == 0
    rows = lambda w: pl.BlockSpec((tm, w), lambda i: (i, 0))
    return pl.pallas_call(
        _outproj_kernel,
        grid=(N // tm,),
        in_specs=[rows(D), rows(a.shape[1]), rows(b.shape[1]), _resident(w_a.shape), _resident(w_b.shape),
                  _resident(g.shape), _resident(be.shape)],
        out_specs=rows(D),
        out_shape=jax.ShapeDtypeStruct((N, D), F32),
        compiler_params=_cparams(("arbitrary",)),
        name="outproj_ln",
    )(x, a, b, w_a, w_b, g, be)


def _ffn_kernel(h_ref, wu_ref, wd_ref, g_ref, be_ref, y_ref, hb_ref, acc_ref, *, last):
    j = pl.program_id(1)

    @pl.when(j == 0)
    def _():
        hb_ref[...] = h_ref[...].astype(BF16)
        acc_ref[...] = jnp.zeros_like(acc_ref)

    u = jnp.dot(hb_ref[...], wu_ref[...], preferred_element_type=F32)
    u = jnp.maximum(u, 0.0)
    acc_ref[...] += jnp.dot((u * u).astype(BF16), wd_ref[...], preferred_element_type=F32)

    @pl.when(j == last)
    def _():
        y_ref[...] = _layer_norm(DN_ALPHA * h_ref[...] + acc_ref[...], g_ref[...], be_ref[...])


def _ffn_ln(h, w_up, w_down, g, be, *, tm, tf):
    N, D = h.shape
    F = w_up.shape[1]
    assert N % tm == 0 and F % tf == 0
    return pl.pallas_call(
        functools.partial(_ffn_kernel, last=F // tf - 1),
        grid=(N // tm, F // tf),
        in_specs=[pl.BlockSpec((tm, D), lambda i, j: (i, 0)),
                  pl.BlockSpec((D, tf), lambda i, j: (0, j)),
                  pl.BlockSpec((tf, D), lambda i, j: (j, 0)),
                  _resident(g.shape), _resident(be.shape)],
        out_specs=pl.BlockSpec((tm, D), lambda i, j: (i, 0)),
        out_shape=jax.ShapeDtypeStruct((N, D), F32),
        scratch_shapes=[pltpu.VMEM((tm, D), BF16), pltpu.VMEM((tm, D), F32)],
        compiler_params=_cparams(("arbitrary", "arbitrary")),
        name="ffn_ln",
    )(h, w_up, w_down, g, be)


def _token_tile(n, cap):
    t = min(n, cap)
    while n % t:
        t //= 2
    return t


def kernel(x_prompt, x_sample, cache_a_k, cache_a_v, cache_b_k, cache_b_v,
           w_in, sinks, w_out, ln1_g, ln1_b, w_up, w_down, ln2_g, ln2_b):
    B, L, D = x_prompt.shape
    DB, T, _ = x_sample.shape
    assert w_in.shape[0] == 1, "one layer"
    perm = _qa_head_perm()

    w_in_bf = jnp.concatenate([w_in[0][:, perm], w_in[0][:, QA:]], axis=1).astype(BF16)
    w_out_a = w_out[0][:QA][perm].astype(BF16)
    w_out_b = w_out[0][QA:].astype(BF16)
    w_up_bf = w_up[0].astype(BF16)
    w_down_bf = w_down[0].astype(BF16)
    g1, b1, g2, b2 = ln1_g[0][None], ln1_b[0][None], ln2_g[0][None], ln2_b[0][None]

    cos_p, sin_p = _rope_tables(jnp.arange(L))
    qa, ka, va, qb, kb, vb, pak, pav, pbk, pbv = _project(
        x_prompt, w_in_bf, cos_p, sin_p, tm=_token_tile(L, 256), out_dtype=BF16, with_tails=True)
    a_p = _band_attention(qa, ka, va, stride=1, inclusive=False, sinks=sinks[0], name="sink_attn")
    state = None
    for i, (window, r) in enumerate(DIL_BRANCHES):
        assert window // r == BLK
        final = i == len(DIL_BRANCHES) - 1
        res = _band_attention(qb, kb, vb, stride=r, inclusive=True, prev=state, emit_state=not final,
                              out_dtype=BF16 if final else F32, name=f"dilated_r{r}")
        if final:
            b_p = res
        else:
            state = res
    N_p = B * L
    h_p = _outproj_ln(x_prompt.reshape(N_p, D), a_p.reshape(N_p, QA), b_p.reshape(N_p, QB),
                      w_out_a, w_out_b, g1, b1, tm=_token_tile(N_p, 512))
    y_p = _ffn_ln(h_p, w_up_bf, w_down_bf, g2, b2, tm=_token_tile(N_p, 512), tf=1024)

    N_s = DB * T
    cos_s, sin_s = _rope_tables(PAST_LEN + jnp.arange(T))
    cos_s = jnp.tile(cos_s, (DB, 1))
    sin_s = jnp.tile(sin_s, (DB, 1))
    sq = _project(x_sample.reshape(1, N_s, D), w_in_bf, cos_s, sin_s, tm=_token_tile(N_s, 256),
                  out_dtype=F32, with_tails=False)
    qa_s, kan, van, qb_s, kbn, vbn = [z.reshape(DB, T, z.shape[-1]) for z in sq]
    n_a, n_b = cache_a_k.shape[2], cache_b_k.shape[2]
    a_s, b_s, sak, sav, sbk, sbv = _sample_attention(
        qa_s, kan, van, qb_s, kbn, vbn,
        cache_a_k[0].reshape(DB, n_a, KA), cache_a_v[0].reshape(DB, n_a, KA),
        cache_b_k[0].reshape(DB, n_b, QB), cache_b_v[0].reshape(DB, n_b, QB), sinks[0])
    h_s = _outproj_ln(x_sample.reshape(N_s, D), a_s.reshape(N_s, QA), b_s.reshape(N_s, QB),
                      w_out_a, w_out_b, g1, b1, tm=_token_tile(N_s, 512))
    y_s = _ffn_ln(h_s, w_up_bf, w_down_bf, g2, b2, tm=_token_tile(N_s, 512), tf=1024)

    rows_a, rows_b = min(WIN_A, L), min(WIN_B, L)
    return (y_p.reshape(B, L, D), y_s.reshape(DB, T, D),
            pak.reshape(1, B, rows_a, N_KV_A, HEAD_DIM), pav.reshape(1, B, rows_a, N_KV_A, HEAD_DIM),
            pbk.reshape(1, B, rows_b, N_HEADS_B, HEAD_DIM), pbv.reshape(1, B, rows_b, N_HEADS_B, HEAD_DIM),
            sak.reshape(1, DB, n_a, N_KV_A, HEAD_DIM), sav.reshape(1, DB, n_a, N_KV_A, HEAD_DIM),
            sbk.reshape(1, DB, n_b, N_HEADS_B, HEAD_DIM), sbv.reshape(1, DB, n_b, N_HEADS_B, HEAD_DIM))
```

```python
import functools

import numpy as np
import jax
import jax.numpy as jnp
from jax import lax
from jax.experimental import pallas as pl
from jax.experimental.pallas import tpu as pltpu

F32 = jnp.float32
BF16 = jnp.bfloat16

HEAD_DIM = 64
LANES = 128
N_HEADS_A = 16
N_KV_A = 4
GROUP_A = N_HEADS_A // N_KV_A
N_HEADS_B = 16
WIN_A = 128
DIL_BRANCHES = ((128, 1), (512, 4), (2048, 16))
WIN_B = 2048
BLK = 128
ROPE_THETA = 10000.0
LN_EPS = 1e-5
DN_ALPHA = 2.0 ** 0.25
SCALE = HEAD_DIM ** -0.5
NEG_INF = -1e30
PAST_LEN = 16384
QA = N_HEADS_A * HEAD_DIM
KA = N_KV_A * HEAD_DIM
QB = N_HEADS_B * HEAD_DIM
IN_WIDTH = QA + 2 * KA + 3 * QB
GROUPS_B = QB // LANES
VMEM_LIMIT = 56 * 1024 * 1024
NT_DIMS = (((1,), (1,)), ((), ()))


def _cparams(sem):
    return pltpu.CompilerParams(dimension_semantics=sem, vmem_limit_bytes=VMEM_LIMIT)


def _resident(shape):
    nd = len(shape)
    return pl.BlockSpec(shape, lambda *_: (0,) * nd, pipeline_mode=pl.Buffered(1))


def _qa_head_perm():
    cols = []
    for p in range(N_KV_A // 2):
        for j in range(GROUP_A):
            for half in range(2):
                head = GROUP_A * (2 * p + half) + j
                cols.extend(range(head * HEAD_DIM, (head + 1) * HEAD_DIM))
    return np.asarray(cols, np.int32)


def _rope_tables(pos):
    half = HEAD_DIM // 2
    inv = 1.0 / (ROPE_THETA ** (jnp.arange(half, dtype=F32) / half))
    ang = pos.astype(F32)[:, None] * inv[None, :]
    cos = jnp.cos(ang)
    sin = jnp.sin(ang)
    return (jnp.concatenate([cos, cos, cos, cos], -1),
            jnp.concatenate([-sin, sin, -sin, sin], -1))


def _lse_expander():
    e = np.zeros((LANES, QB), np.float32)
    for n in range(GROUPS_B):
        e[n, n * LANES:n * LANES + HEAD_DIM] = 1.0
        e[HEAD_DIM + n, n * LANES + HEAD_DIM:(n + 1) * LANES] = 1.0
    return e


def _proj_kernel(x_ref, w_ref, cos_ref, sin_ref, *refs, tm, spread, cache_rows_a, first_cache_tile, last_tile):
    qa_ref, ka_ref, va_ref = refs[:3]
    nat_b = refs[3:6]
    if spread:
        by4, by16, tails, (s1_ref, s2_ref) = refs[6:9], refs[9:12], refs[12:16], refs[16:18]
    t = pl.program_id(1)
    xb = x_ref[0].astype(BF16)
    cos = cos_ref[...]
    sin = sin_ref[...]
    lane = lax.broadcasted_iota(jnp.int32, (tm, LANES), 1)
    lo_half = (lane % HEAD_DIM) < (HEAD_DIM // 2)

    def rope(hg):
        swapped = jnp.where(lo_half, pltpu.roll(hg, LANES - HEAD_DIM // 2, 1), pltpu.roll(hg, HEAD_DIM // 2, 1))
        return hg * cos + swapped * sin

    def groups(col0, n):
        h = jnp.dot(xb, w_ref[:, col0:col0 + n * LANES], preferred_element_type=F32)
        return [h[:, g * LANES:(g + 1) * LANES] for g in range(n)]

    def put(ref, g, val):
        ref[0, :, g * LANES:(g + 1) * LANES] = val.astype(ref.dtype)

    def put_spread(which, g, val):
        slab = which * GROUPS_B + g
        n4, n16 = tm // 4, tm // 16
        s1_ref[slab] = val
        for r4 in range(4):
            part = s1_ref[slab, pl.ds(r4, n4, stride=4), :]
            by4[which][0, :, r4 * QB + g * LANES:r4 * QB + (g + 1) * LANES] = part.astype(BF16)
            s2_ref[slab, r4 * n4:(r4 + 1) * n4, :] = part
        for r4 in range(4):
            for a in range(4):
                part = s2_ref[slab, pl.ds(r4 * n4 + a, n16, stride=4), :]
                r16 = 4 * a + r4
                by16[which][0, :, r16 * QB + g * LANES:r16 * QB + (g + 1) * LANES] = part.astype(BF16)

    for c in range(QA // 512):
        for g, hg in enumerate(groups(c * 512, 4)):
            put(qa_ref, c * 4 + g, rope(hg) * SCALE)

    kva = groups(QA, 4)
    ka_g = [rope(kva[0]), rope(kva[1])]
    va_g = [kva[2], kva[3]]
    for g in range(2):
        put(ka_ref, g, ka_g[g])
        put(va_ref, g, va_g[g])
    if spread:
        @pl.when(t == last_tile)
        def _():
            for g in range(2):
                tails[0][0, g * LANES:(g + 1) * LANES, :] = ka_g[g][tm - cache_rows_a:, :].T
                tails[1][0, g * LANES:(g + 1) * LANES, :] = va_g[g][tm - cache_rows_a:, :].T

    base = QA + 2 * KA
    for which in range(3):
        for c in range(QB // 512):
            vals = groups(base + QB * which + c * 512, 4)
            if which < 2:
                vals = [rope(v) for v in vals]
            if which == 0:
                vals = [v * SCALE for v in vals]
            for g, v in enumerate(vals):
                put(nat_b[which], c * 4 + g, v)
                if spread:
                    put_spread(which, c * 4 + g, v)
            if spread and which > 0:
                @pl.when(t >= first_cache_tile)
                def _(vals=vals, c=c, which=which):
                    for g, v in enumerate(vals):
                        tails[1 + which][0, (c * 4 + g) * LANES:(c * 4 + g + 1) * LANES, :] = v.T


def _project(x, w_in_bf, cos, sin, *, tm, out_dtype, spread):
    B, L, D = x.shape
    nt = L // tm
    assert L % tm == 0
    widths = (QA, KA, KA, QB, QB, QB)
    out_shape = [jax.ShapeDtypeStruct((B, L, w), out_dtype) for w in widths]
    out_specs = [pl.BlockSpec((1, tm, w), lambda b, t: (b, t, 0)) for w in widths]
    scratch = []
    rows_a = min(WIN_A, L)
    rows_b = min(WIN_B, L)
    first_cache_tile = (L - rows_b) // tm
    if spread:
        assert tm % 256 == 0 and rows_a <= tm and (L - rows_b) % tm == 0
        for r in (4, 16):
            out_shape += [jax.ShapeDtypeStruct((B, L // r, r * QB), BF16)] * 3
            out_specs += [pl.BlockSpec((1, tm // r, r * QB), lambda b, t: (b, t, 0))] * 3
        out_shape += [jax.ShapeDtypeStruct((B, KA, rows_a), F32)] * 2
        out_shape += [jax.ShapeDtypeStruct((B, QB, rows_b), F32)] * 2
        out_specs += [pl.BlockSpec((1, KA, rows_a), lambda b, t: (b, 0, 0))] * 2
        out_specs += [pl.BlockSpec((1, QB, tm), lambda b, t: (b, 0, jnp.maximum(t - first_cache_tile, 0)))] * 2
        scratch = [pltpu.VMEM((3 * GROUPS_B, tm, LANES), F32)] * 2
    kern = functools.partial(_proj_kernel, tm=tm, spread=spread, cache_rows_a=rows_a,
                             first_cache_tile=first_cache_tile, last_tile=nt - 1)
    return pl.pallas_call(
        kern,
        grid=(B, nt),
        in_specs=[pl.BlockSpec((1, tm, D), lambda b, t: (b, t, 0)),
                  _resident((D, IN_WIDTH)),
                  pl.BlockSpec((tm, LANES), lambda b, t: (t, 0)),
                  pl.BlockSpec((tm, LANES), lambda b, t: (t, 0))],
        out_specs=out_specs,
        out_shape=out_shape,
        scratch_shapes=scratch,
        compiler_params=_cparams(("arbitrary", "arbitrary")),
        name="proj_spread" if spread else "proj",
    )(x, w_in_bf, cos, sin)


def _band_attn_kernel(*refs, n_kgroups, q_per_k, inclusive, has_sink, emit_lse):
    it = iter(refs)
    q_ref, kp_ref, kc_ref, vp_ref, vc_ref = (next(it) for _ in range(5))
    sink_ref = next(it) if has_sink else None
    o_ref = next(it)
    lse_ref = next(it) if emit_lse else None

    mt = pl.program_id(2)
    row = lax.broadcasted_iota(jnp.int32, (2 * BLK, 2 * BLK), 0)
    col = lax.broadcasted_iota(jnp.int32, (2 * BLK, 2 * BLK), 1)
    dist = (row % BLK) + BLK - col
    valid = (dist >= 0) & ((dist <= BLK) if inclusive else (dist < BLK)) & ((col >= BLK) | (mt > 0))
    lane = lax.broadcasted_iota(jnp.int32, (BLK, LANES), 1)
    lane_lo = lane < HEAD_DIM
    upper_rows = lax.broadcasted_iota(jnp.int32, (2 * BLK, 1), 0) >= BLK
    ones = jnp.ones((2 * BLK, LANES), BF16)
    zero_q = jnp.zeros((BLK, LANES), BF16)
    lse_c = jnp.zeros((BLK, LANES), F32)

    for n in range(n_kgroups):
        ksl = slice(n * LANES, (n + 1) * LANES)
        k2 = jnp.concatenate([kp_ref[0, :, ksl], kc_ref[0, :, ksl]], axis=0)
        v2 = jnp.concatenate([vp_ref[0, :, ksl], vc_ref[0, :, ksl]], axis=0)
        v2e = jnp.concatenate([v2, ones], axis=1)
        for j in range(q_per_k):
            g = n * q_per_k + j
            gsl = slice(g * LANES, (g + 1) * LANES)
            qg = q_ref[0, :, gsl]
            qs = jnp.concatenate([jnp.where(lane_lo, qg, zero_q), jnp.where(lane_lo, zero_q, qg)], axis=0)
            s = lax.dot_general(qs, k2, NT_DIMS, preferred_element_type=F32)
            s = jnp.where(valid, s, NEG_INF)
            m = jnp.max(s, axis=1, keepdims=True)
            if has_sink:
                lo_head = 2 * GROUP_A * n + j
                sink = jnp.where(upper_rows, sink_ref[lo_head + GROUP_A], sink_ref[lo_head])
                m = jnp.maximum(m, sink)
            p = jnp.exp(s - m)
            r = jnp.dot(p.astype(BF16), v2e, preferred_element_type=F32)
            l = r[:, LANES:]
            if has_sink:
                l = l + jnp.exp(sink - m)
            on = r[:, :LANES] / l
            o_ref[0, :, gsl] = jnp.where(lane_lo, on[:BLK], on[BLK:]).astype(o_ref.dtype)
            if emit_lse:
                lse_rows = m + jnp.log(l)
                lse = jnp.where(lane_lo, lse_rows[:BLK], lse_rows[BLK:])
                lse_c = jnp.where((lane == g) | (lane == HEAD_DIM + g), lse, lse_c)
    if emit_lse:
        lse_ref[0] = lse_c


def _band_attention(q, k, v, *, stride, inclusive, sinks=None, emit_lse=False, name="band_attn"):
    B, M, _ = q.shape
    Wq, Wk = q.shape[-1] // stride, k.shape[-1] // stride
    assert M % BLK == 0
    cur = lambda w: pl.BlockSpec((1, BLK, w), lambda b, r, m: (b, m, r))
    prv = lambda w: pl.BlockSpec((1, BLK, w), lambda b, r, m: (b, jnp.maximum(m - 1, 0), r))
    args = [q, k, k, v, v]
    in_specs = [cur(Wq), prv(Wk), cur(Wk), prv(Wk), cur(Wk)]
    if sinks is not None:
        args.append(sinks)
        in_specs.append(pl.BlockSpec(memory_space=pltpu.SMEM))
    out_shape = [jax.ShapeDtypeStruct(q.shape, BF16)]
    out_specs = [cur(Wq)]
    if emit_lse:
        out_shape.append(jax.ShapeDtypeStruct((B, M, stride * LANES), F32))
        out_specs.append(cur(LANES))
    kern = functools.partial(_band_attn_kernel, n_kgroups=Wk // LANES, q_per_k=Wq // Wk, inclusive=inclusive,
                             has_sink=sinks is not None, emit_lse=emit_lse)
    outs = pl.pallas_call(
        kern,
        grid=(B, stride, M // BLK),
        in_specs=in_specs,
        out_specs=out_specs,
        out_shape=out_shape,
        compiler_params=_cparams(("arbitrary",) * 3),
        name=name,
    )(*args)
    return outs if emit_lse else outs[0]


def _sample_masks(T, n_past_a, n_past_b):
    t = np.arange(T)[:, None]
    dist_c = n_past_a + t - np.arange(n_past_a)[None, :]
    dist_n = t - np.arange(LANES)[None, :]
    a_c = ((dist_c >= 0) & (dist_c < WIN_A)).astype(np.float32)
    a_n = ((dist_n >= 0) & (dist_n < WIN_A)).astype(np.float32)
    dist_c = n_past_b + t - np.arange(n_past_b)[None, :]
    b_c = np.zeros((T, n_past_b), np.float32)
    b_n = np.zeros((T, LANES), np.float32)
    for window, r in DIL_BRANCHES:
        b_c += ((dist_c >= 0) & (dist_c % r == 0) & (dist_c // r <= window // r)).astype(np.float32)
        b_n += ((dist_n >= 0) & (dist_n % r == 0) & (dist_n // r <= window // r)).astype(np.float32)
    tile = lambda m, reps: np.tile(m, (reps, 1))
    return (tile(a_c, 2 * GROUP_A), tile(a_n, 2 * GROUP_A), tile(b_c, N_HEADS_B // 2), tile(b_n, N_HEADS_B // 2))


def _sample_kernel(qa_ref, kan_ref, van_ref, qb_ref, kbn_ref, vbn_ref, cak_ref, cav_ref, cbk_ref, cbv_ref,
                   sink_ref, mac_ref, man_ref, mbc_ref, mbn_ref,
                   a_ref, b_ref, oak_ref, oav_ref, obk_ref, obv_ref, *, T):
    n_past_a = cak_ref.shape[2]
    n_past_b = cbk_ref.shape[2]
    half_w = QB // 2

    def pad_rows(x, top):
        z = jnp.zeros((LANES - T, x.shape[1]), F32)
        return jnp.concatenate([x, z] if top else [z, x], axis=0)

    def rolled(cache, new_t):
        n = cache.shape[1]
        out = pltpu.roll(cache, n - T, 1)
        last = jnp.where(lax.broadcasted_iota(jnp.int32, (cache.shape[0], LANES), 1) < LANES - T,
                         out[:, n - LANES:], new_t)
        return out, last

    def softmax_parts(s_c, s_n, m_c, m_n, sink=None):
        s_c = jnp.where(m_c > 0, s_c, NEG_INF)
        s_n = jnp.where(m_n > 0, s_n, NEG_INF)
        m = jnp.maximum(jnp.max(s_c, axis=1, keepdims=True), jnp.max(s_n, axis=1, keepdims=True))
        if sink is not None:
            m = jnp.maximum(m, sink)
        p_c = m_c * jnp.exp(s_c - m)
        p_n = m_n * jnp.exp(s_n - m)
        l = jnp.sum(p_c, axis=1, keepdims=True) + jnp.sum(p_n, axis=1, keepdims=True)
        if sink is not None:
            l = l + jnp.exp(sink - m)
        return p_c, p_n, l

    for cache_ref, new_ref, out_ref in ((cak_ref, kan_ref, oak_ref), (cav_ref, van_ref, oav_ref)):
        out, last = rolled(cache_ref[0], pad_rows(new_ref[0], top=False).T)
        if n_past_a > LANES:
            out_ref[0, :, :n_past_a - LANES] = out[:, :n_past_a - LANES]
        out_ref[0, :, n_past_a - LANES:] = last

    lane_a = lax.broadcasted_iota(jnp.int32, (T, LANES), 1)
    lo_a = lane_a < HEAD_DIM
    blocks = []
    for j in range(GROUP_A):
        qg = qa_ref[0, :, j * LANES:(j + 1) * LANES]
        blocks += [jnp.where(lo_a, qg, 0.0), jnp.where(lo_a, 0.0, qg)]
    qs = jnp.concatenate(blocks, axis=0).astype(BF16)
    s_c = jnp.dot(qs, cak_ref[0].astype(BF16), preferred_element_type=F32)
    s_n = lax.dot_general(qs, pad_rows(kan_ref[0], top=True).astype(BF16), NT_DIMS, preferred_element_type=F32)
    p_c, p_n, l = softmax_parts(s_c, s_n, mac_ref[...], man_ref[...], sink_ref[0][:, 0:1])
    r = lax.dot_general(p_c.astype(BF16), cav_ref[0].astype(BF16), NT_DIMS, preferred_element_type=F32)
    r = r + jnp.dot(p_n.astype(BF16), pad_rows(van_ref[0], top=True).astype(BF16), preferred_element_type=F32)
    r = r / l
    for j in range(GROUP_A):
        lo_rows = r[(2 * j) * T:(2 * j + 1) * T]
        hi_rows = r[(2 * j + 1) * T:(2 * j + 2) * T]
        a_ref[0, :, j * LANES:(j + 1) * LANES] = jnp.where(lo_a, lo_rows, hi_rows)

    n_heads = half_w // HEAD_DIM
    rows_b = n_heads * T
    row_head = lax.broadcasted_iota(jnp.int32, (rows_b, half_w), 0) // T
    lane_head = lax.broadcasted_iota(jnp.int32, (rows_b, half_w), 1) // HEAD_DIM
    diag = row_head == lane_head
    qblk = jnp.where(diag, jnp.concatenate([qb_ref[0]] * n_heads, axis=0), 0.0).astype(BF16)
    s_c = jnp.dot(qblk, cbk_ref[0].astype(BF16), preferred_element_type=F32)
    s_n = lax.dot_general(qblk, pad_rows(kbn_ref[0], top=True).astype(BF16), NT_DIMS, preferred_element_type=F32)
    p_c, p_n, l = softmax_parts(s_c, s_n, mbc_ref[...], mbn_ref[...])
    r = lax.dot_general(p_c.astype(BF16), cbv_ref[0].astype(BF16), NT_DIMS, preferred_element_type=F32)
    r = r + jnp.dot(p_n.astype(BF16), pad_rows(vbn_ref[0], top=True).astype(BF16), preferred_element_type=F32)
    r = jnp.where(diag, r / l, 0.0)
    out = r[0:T]
    for h in range(1, n_heads):
        out = out + r[h * T:(h + 1) * T]
    b_ref[0] = out

    for cache_ref, new_ref, out_ref in ((cbk_ref, kbn_ref, obk_ref), (cbv_ref, vbn_ref, obv_ref)):
        new_t = pad_rows(new_ref[0], top=False).T
        for h in range(n_heads):
            rows = slice(h * HEAD_DIM, (h + 1) * HEAD_DIM)
            out, last = rolled(cache_ref[0, rows, :], new_t[rows])
            out_ref[0, rows, :n_past_b - LANES] = out[:, :n_past_b - LANES]
            out_ref[0, rows, n_past_b - LANES:] = last


def _sample_attention(qa, kan, van, qb, kbn, vbn, cak, cav, cbk, cbv, sinks):
    DB, T, _ = qa.shape
    n_past_a, n_past_b = cak.shape[2], cbk.shape[2]
    assert T % 8 == 0 and T <= LANES and n_past_a % LANES == 0 and n_past_b % LANES == 0 and n_past_b > LANES
    masks = [jnp.asarray(m) for m in _sample_masks(T, n_past_a, n_past_b)]
    heads = np.asarray([[GROUP_A * (2 * p + half) + j for j in range(GROUP_A) for half in range(2) for _ in range(T)]
                        for p in range(N_KV_A // 2)], np.int32)
    sink_rows = jnp.broadcast_to(sinks[heads][:, :, None], heads.shape + (LANES,)).astype(F32)

    new = lambda w: pl.BlockSpec((1, T, w), lambda b, p: (b, 0, p))
    cache = lambda w, n: pl.BlockSpec((1, w, n), lambda b, p: (b, p, 0))
    in_specs = [new(QA // 2), new(KA // 2), new(KA // 2), new(QB // 2), new(QB // 2), new(QB // 2),
                cache(KA // 2, n_past_a), cache(KA // 2, n_past_a), cache(QB // 2, n_past_b), cache(QB // 2, n_past_b),
                pl.BlockSpec((1,) + sink_rows.shape[1:], lambda b, p: (p, 0, 0))] + [_resident(m.shape) for m in masks]
    out_shape = [jax.ShapeDtypeStruct((DB, T, QA), F32), jax.ShapeDtypeStruct((DB, T, QB), F32),
                 jax.ShapeDtypeStruct(cak.shape, F32), jax.ShapeDtypeStruct(cav.shape, F32),
                 jax.ShapeDtypeStruct(cbk.shape, F32), jax.ShapeDtypeStruct(cbv.shape, F32)]
    out_specs = [new(QA // 2), new(QB // 2), cache(KA // 2, n_past_a), cache(KA // 2, n_past_a),
                 cache(QB // 2, n_past_b), cache(QB // 2, n_past_b)]
    return pl.pallas_call(
        functools.partial(_sample_kernel, T=T),
        grid=(DB, 2),
        in_specs=in_specs,
        out_specs=out_specs,
        out_shape=out_shape,
        compiler_params=_cparams(("arbitrary", "arbitrary")),
        name="sample_attn",
    )(qa, kan, van, qb, kbn, vbn, cak, cav, cbk, cbv, sink_rows, *masks)


def _layer_norm(z, g, b):
    mu = jnp.mean(z, axis=-1, keepdims=True)
    zc = z - mu
    var = jnp.mean(zc * zc, axis=-1, keepdims=True)
    return zc * lax.rsqrt(var + LN_EPS) * g + b


def _outproj_kernel(x_ref, a_ref, b_ref, wa_ref, wb_ref, g_ref, be_ref, h_ref):
    proj = jnp.dot(a_ref[...].astype(BF16), wa_ref[...], preferred_element_type=F32)
    proj = proj + jnp.dot(b_ref[...].astype(BF16), wb_ref[...], preferred_element_type=F32)
    h_ref[...] = _layer_norm(DN_ALPHA * x_ref[...] + proj, g_ref[...], be_ref[...])


def _outproj_ln(x, a, b, w_a, w_b, g, be, *, tm):
    N, D = x.shape
    assert N % tm == 0
    rows = lambda w: pl.BlockSpec((tm, w), lambda i: (i, 0))
    return pl.pallas_call(
        _outproj_kernel,
        grid=(N // tm,),
        in_specs=[rows(D), rows(a.shape[1]), rows(b.shape[1]), _resident(w_a.shape), _resident(w_b.shape),
                  _resident(g.shape), _resident(be.shape)],
        out_specs=rows(D),
        out_shape=jax.ShapeDtypeStruct((N, D), F32),
        compiler_params=_cparams(("arbitrary",)),
        name="outproj_ln",
    )(x, a, b, w_a, w_b, g, be)


def _merge_outproj_kernel(x_ref, a_ref, o1_ref, l1_ref, o4_ref, l4_ref, o16_ref, l16_ref, e_ref,
                          wa_ref, wb_ref, g_ref, be_ref, h_ref, n4_ref, n16_ref, nl_ref, *, tm):
    m4, m16 = tm // 4, tm // 16
    for r in range(4):
        nl_ref[0, pl.ds(r, m4, stride=4), :] = l4_ref[:, r * LANES:(r + 1) * LANES]
        for g in range(GROUPS_B):
            n4_ref[g, pl.ds(r, m4, stride=4), :] = o4_ref[:, r * QB + g * LANES:r * QB + (g + 1) * LANES].astype(F32)
    for r in range(16):
        nl_ref[1, pl.ds(r, m16, stride=16), :] = l16_ref[:, r * LANES:(r + 1) * LANES]
        for g in range(GROUPS_B):
            n16_ref[g, pl.ds(r, m16, stride=16), :] = (
                o16_ref[:, r * QB + g * LANES:r * QB + (g + 1) * LANES].astype(F32))

    lses = [l1_ref[...], nl_ref[0], nl_ref[1]]
    mx = jnp.maximum(jnp.maximum(lses[0], lses[1]), lses[2])
    ws = [jnp.exp(l - mx) for l in lses]
    inv = 1.0 / (ws[0] + ws[1] + ws[2])
    e = e_ref[...]

    def widen(w):
        hi = w.astype(BF16)
        lo = (w - hi.astype(F32)).astype(BF16)
        return jnp.dot(hi, e, preferred_element_type=F32) + jnp.dot(lo, e, preferred_element_type=F32)

    ws = [widen(w * inv) for w in ws]
    merged = []
    for g in range(GROUPS_B):
        sl = slice(g * LANES, (g + 1) * LANES)
        bg = ws[0][:, sl] * o1_ref[:, sl].astype(F32) + ws[1][:, sl] * n4_ref[g] + ws[2][:, sl] * n16_ref[g]
        merged.append(bg.astype(BF16))
    proj = jnp.dot(a_ref[...], wa_ref[...], preferred_element_type=F32)
    proj = proj + jnp.dot(jnp.concatenate(merged, axis=1), wb_ref[...], preferred_element_type=F32)
    h_ref[...] = _layer_norm(DN_ALPHA * x_ref[...] + proj, g_ref[...], be_ref[...])


def _merge_outproj_ln(x, a, o1, l1, o4, l4, o16, l16, w_a, w_b, g, be, *, tm):
    N, D = x.shape
    assert N % tm == 0 and tm % 256 == 0
    rows = lambda n, w: pl.BlockSpec((n, w), lambda i: (i, 0))
    expander = jnp.asarray(_lse_expander(), BF16)
    return pl.pallas_call(
        functools.partial(_merge_outproj_kernel, tm=tm),
        grid=(N // tm,),
        in_specs=[rows(tm, D), rows(tm, QA), rows(tm, QB), rows(tm, LANES),
                  rows(tm // 4, 4 * QB), rows(tm // 4, 4 * LANES), rows(tm // 16, 16 * QB), rows(tm // 16, 16 * LANES),
                  _resident(expander.shape), _resident(w_a.shape), _resident(w_b.shape),
                  _resident(g.shape), _resident(be.shape)],
        out_specs=rows(tm, D),
        out_shape=jax.ShapeDtypeStruct((N, D), F32),
        scratch_shapes=[pltpu.VMEM((GROUPS_B, tm, LANES), F32), pltpu.VMEM((GROUPS_B, tm, LANES), F32),
                        pltpu.VMEM((2, tm, LANES), F32)],
        compiler_params=_cparams(("arbitrary",)),
        name="merge_outproj_ln",
    )(x, a, o1, l1, o4, l4, o16, l16, expander, w_a, w_b, g, be)


def _ffn_kernel(h_ref, wu_ref, wd_ref, g_ref, be_ref, y_ref, hb_ref, acc_ref, *, last):
    j = pl.program_id(1)

    @pl.when(j == 0)
    def _():
        hb_ref[...] = h_ref[...].astype(BF16)
        acc_ref[...] = jnp.zeros_like(acc_ref)

    u = jnp.dot(hb_ref[...], wu_ref[...], preferred_element_type=F32)
    u = jnp.maximum(u, 0.0)
    acc_ref[...] += jnp.dot((u * u).astype(BF16), wd_ref[...], preferred_element_type=F32)

    @pl.when(j == last)
    def _():
        y_ref[...] = _layer_norm(DN_ALPHA * h_ref[...] + acc_ref[...], g_ref[...], be_ref[...])


def _ffn_ln(h, w_up, w_down, g, be, *, tm, tf):
    N, D = h.shape
    F = w_up.shape[1]
    assert N % tm == 0 and F % tf == 0
    return pl.pallas_call(
        functools.partial(_ffn_kernel, last=F // tf - 1),
        grid=(N // tm, F // tf),
        in_specs=[pl.BlockSpec((tm, D), lambda i, j: (i, 0)),
                  pl.BlockSpec((D, tf), lambda i, j: (0, j)),
                  pl.BlockSpec((tf, D), lambda i, j: (j, 0)),
                  _resident(g.shape), _resident(be.shape)],
        out_specs=pl.BlockSpec((tm, D), lambda i, j: (i, 0)),
        out_shape=jax.ShapeDtypeStruct((N, D), F32),
        scratch_shapes=[pltpu.VMEM((tm, D), BF16), pltpu.VMEM((tm, D), F32)],
        compiler_params=_cparams(("arbitrary", "arbitrary")),
        name="ffn_ln",
    )(h, w_up, w_down, g, be)


def _token_tile(n, cap):
    t = min(n, cap)
    while n % t:
        t //= 2
    return t


def _position_minor(cache):
    _, DB, n, H, Dh = cache.shape
    return jnp.transpose(cache[0], (0, 2, 3, 1)).reshape(DB, H * Dh, n)


def _position_major(x, heads):
    DB, _, n = x.shape
    return jnp.transpose(x.reshape(DB, heads, HEAD_DIM, n), (0, 3, 1, 2))[None]


def kernel(x_prompt, x_sample, cache_a_k, cache_a_v, cache_b_k, cache_b_v,
           w_in, sinks, w_out, ln1_g, ln1_b, w_up, w_down, ln2_g, ln2_b):
    B, L, D = x_prompt.shape
    DB, T, _ = x_sample.shape
    assert w_in.shape[0] == 1, "one layer"
    assert DIL_BRANCHES == ((BLK, 1), (4 * BLK, 4), (16 * BLK, 16))
    perm = _qa_head_perm()

    w_in_bf = jnp.concatenate([w_in[0][:, perm], w_in[0][:, QA:]], axis=1).astype(BF16)
    w_out_a = w_out[0][:QA][perm].astype(BF16)
    w_out_b = w_out[0][QA:].astype(BF16)
    w_up_bf = w_up[0].astype(BF16)
    w_down_bf = w_down[0].astype(BF16)
    g1, b1, g2, b2 = ln1_g[0][None], ln1_b[0][None], ln2_g[0][None], ln2_b[0][None]

    cos_p, sin_p = _rope_tables(jnp.arange(L))
    (qa, ka, va, qb1, kb1, vb1, qb4, kb4, vb4, qb16, kb16, vb16, pak, pav, pbk, pbv) = _project(
        x_prompt, w_in_bf, cos_p, sin_p, tm=_token_tile(L, 256), out_dtype=BF16, spread=True)
    a_p = _band_attention(qa, ka, va, stride=1, inclusive=False, sinks=sinks[0], name="sink_attn")
    o1, l1 = _band_attention(qb1, kb1, vb1, stride=1, inclusive=True, emit_lse=True, name="dilated_r1")
    o4, l4 = _band_attention(qb4, kb4, vb4, stride=4, inclusive=True, emit_lse=True, name="dilated_r4")
    o16, l16 = _band_attention(qb16, kb16, vb16, stride=16, inclusive=True, emit_lse=True, name="dilated_r16")
    N_p = B * L
    flat = lambda z: z.reshape(-1, z.shape[-1])
    h_p = _merge_outproj_ln(flat(x_prompt), flat(a_p), flat(o1), flat(l1), flat(o4), flat(l4), flat(o16), flat(l16),
                            w_out_a, w_out_b, g1, b1, tm=_token_tile(L, 256))
    y_p = _ffn_ln(h_p, w_up_bf, w_down_bf, g2, b2, tm=_token_tile(N_p, 512), tf=1024)

    N_s = DB * T
    cos_s, sin_s = _rope_tables(PAST_LEN + jnp.arange(T))
    cos_s = jnp.tile(cos_s, (DB, 1))
    sin_s = jnp.tile(sin_s, (DB, 1))
    sq = _project(x_sample.reshape(1, N_s, D), w_in_bf, cos_s, sin_s, tm=_token_tile(N_s, 256),
                  out_dtype=F32, spread=False)
    qa_s, kan, van, qb_s, kbn, vbn = [z.reshape(DB, T, z.shape[-1]) for z in sq]
    a_s, b_s, sak, sav, sbk, sbv = _sample_attention(
        qa_s, kan, van, qb_s, kbn, vbn,
        _position_minor(cache_a_k), _position_minor(cache_a_v),
        _position_minor(cache_b_k), _position_minor(cache_b_v), sinks[0])
    h_s = _outproj_ln(x_sample.reshape(N_s, D), a_s.reshape(N_s, QA), b_s.reshape(N_s, QB),
                      w_out_a, w_out_b, g1, b1, tm=_token_tile(N_s, 512))
    y_s = _ffn_ln(h_s, w_up_bf, w_down_bf, g2, b2, tm=_token_tile(N_s, 512), tf=1024)

    return (y_p.reshape(B, L, D), y_s.reshape(DB, T, D),
            _position_major(pak, N_KV_A), _position_major(pav, N_KV_A),
            _position_major(pbk, N_HEADS_B), _position_major(pbv, N_HEADS_B),
            _position_major(sak, N_KV_A), _position_major(sav, N_KV_A),
            _position_major(sbk, N_HEADS_B), _position_major(sbv, N_HEADS_B))
```

```python
import functools

import numpy as np
import jax
import jax.numpy as jnp
from jax import lax
from jax.experimental import pallas as pl
from jax.experimental.pallas import tpu as pltpu

F32 = jnp.float32
BF16 = jnp.bfloat16

HEAD_DIM = 64
LANES = 128
N_HEADS_A = 16
N_KV_A = 4
GROUP_A = N_HEADS_A // N_KV_A
N_HEADS_B = 16
WIN_A = 128
DIL_BRANCHES = ((128, 1), (512, 4), (2048, 16))
WIN_B = 2048
BLK = 128
ROPE_THETA = 10000.0
LN_EPS = 1e-5
DN_ALPHA = 2.0 ** 0.25
SCALE = HEAD_DIM ** -0.5
NEG_INF = -1e30
PAST_LEN = 16384
QA = N_HEADS_A * HEAD_DIM
KA = N_KV_A * HEAD_DIM
QB = N_HEADS_B * HEAD_DIM
IN_WIDTH = QA + 2 * KA + 3 * QB
GROUPS_B = QB // LANES
VMEM_LIMIT = 56 * 1024 * 1024
NT_DIMS = (((1,), (1,)), ((), ()))


def _cparams(sem):
    return pltpu.CompilerParams(dimension_semantics=sem, vmem_limit_bytes=VMEM_LIMIT)


def _resident(shape):
    nd = len(shape)
    return pl.BlockSpec(shape, lambda *_: (0,) * nd, pipeline_mode=pl.Buffered(1))


def _qa_head_perm():
    cols = []
    for p in range(N_KV_A // 2):
        for j in range(GROUP_A):
            for half in range(2):
                head = GROUP_A * (2 * p + half) + j
                cols.extend(range(head * HEAD_DIM, (head + 1) * HEAD_DIM))
    return np.asarray(cols, np.int32)


def _rope_tables(pos):
    half = HEAD_DIM // 2
    inv = 1.0 / (ROPE_THETA ** (jnp.arange(half, dtype=F32) / half))
    ang = pos.astype(F32)[:, None] * inv[None, :]
    cos = jnp.cos(ang)
    sin = jnp.sin(ang)
    return (jnp.concatenate([cos, cos, cos, cos], -1),
            jnp.concatenate([-sin, sin, -sin, sin], -1))


def _lse_expander():
    e = np.zeros((LANES, QB), np.float32)
    for n in range(GROUPS_B):
        e[n, n * LANES:n * LANES + HEAD_DIM] = 1.0
        e[HEAD_DIM + n, n * LANES + HEAD_DIM:(n + 1) * LANES] = 1.0
    return e


def _proj_kernel(x_ref, w_ref, cos_ref, sin_ref, *refs, tm, spread, cache_rows_a, first_cache_tile, last_tile):
    qa_ref, ka_ref, va_ref = refs[:3]
    nat_b = refs[3:6]
    if spread:
        by4, by16, tails, (s1_ref, s2_ref) = refs[6:9], refs[9:12], refs[12:16], refs[16:18]
    t = pl.program_id(1)
    xb = x_ref[0].astype(BF16)
    cos = cos_ref[...]
    sin = sin_ref[...]
    lane = lax.broadcasted_iota(jnp.int32, (tm, LANES), 1)
    lo_half = (lane % HEAD_DIM) < (HEAD_DIM // 2)

    def rope(hg):
        swapped = jnp.where(lo_half, pltpu.roll(hg, LANES - HEAD_DIM // 2, 1), pltpu.roll(hg, HEAD_DIM // 2, 1))
        return hg * cos + swapped * sin

    def groups(col0, n):
        h = jnp.dot(xb, w_ref[:, col0:col0 + n * LANES], preferred_element_type=F32)
        return [h[:, g * LANES:(g + 1) * LANES] for g in range(n)]

    def put(ref, g, val):
        ref[0, :, g * LANES:(g + 1) * LANES] = val.astype(ref.dtype)

    def put_spread(which, g, val):
        slab = which * GROUPS_B + g
        n4, n16 = tm // 4, tm // 16
        s1_ref[slab] = val
        for r4 in range(4):
            part = s1_ref[slab, pl.ds(r4, n4, stride=4), :]
            by4[which][0, :, r4 * QB + g * LANES:r4 * QB + (g + 1) * LANES] = part.astype(BF16)
            s2_ref[slab, r4 * n4:(r4 + 1) * n4, :] = part
        for r4 in range(4):
            for a in range(4):
                part = s2_ref[slab, pl.ds(r4 * n4 + a, n16, stride=4), :]
                r16 = 4 * a + r4
                by16[which][0, :, r16 * QB + g * LANES:r16 * QB + (g + 1) * LANES] = part.astype(BF16)

    for c in range(QA // 512):
        for g, hg in enumerate(groups(c * 512, 4)):
            put(qa_ref, c * 4 + g, rope(hg) * SCALE)

    kva = groups(QA, 4)
    ka_g = [rope(kva[0]), rope(kva[1])]
    va_g = [kva[2], kva[3]]
    for g in range(2):
        put(ka_ref, g, ka_g[g])
        put(va_ref, g, va_g[g])
    if spread:
        @pl.when(t == last_tile)
        def _():
            for g in range(2):
                tails[0][0, g * LANES:(g + 1) * LANES, :] = ka_g[g][tm - cache_rows_a:, :].T
                tails[1][0, g * LANES:(g + 1) * LANES, :] = va_g[g][tm - cache_rows_a:, :].T

    base = QA + 2 * KA
    for which in range(3):
        for c in range(QB // 512):
            vals = groups(base + QB * which + c * 512, 4)
            if which < 2:
                vals = [rope(v) for v in vals]
            if which == 0:
                vals = [v * SCALE for v in vals]
            for g, v in enumerate(vals):
                put(nat_b[which], c * 4 + g, v)
                if spread:
                    put_spread(which, c * 4 + g, v)
            if spread and which > 0:
                @pl.when(t >= first_cache_tile)
                def _(vals=vals, c=c, which=which):
                    for g, v in enumerate(vals):
                        tails[1 + which][0, (c * 4 + g) * LANES:(c * 4 + g + 1) * LANES, :] = v.T


def _project(x, w_in_bf, cos, sin, *, tm, out_dtype, spread):
    B, L, D = x.shape
    nt = L // tm
    assert L % tm == 0
    widths = (QA, KA, KA, QB, QB, QB)
    out_shape = [jax.ShapeDtypeStruct((B, L, w), out_dtype) for w in widths]
    out_specs = [pl.BlockSpec((1, tm, w), lambda b, t: (b, t, 0)) for w in widths]
    scratch = []
    rows_a = min(WIN_A, L)
    rows_b = min(WIN_B, L)
    first_cache_tile = (L - rows_b) // tm
    if spread:
        assert tm % 256 == 0 and rows_a <= tm and (L - rows_b) % tm == 0
        for r in (4, 16):
            out_shape += [jax.ShapeDtypeStruct((B, L // r, r * QB), BF16)] * 3
            out_specs += [pl.BlockSpec((1, tm // r, r * QB), lambda b, t: (b, t, 0))] * 3
        out_shape += [jax.ShapeDtypeStruct((B, KA, rows_a), F32)] * 2
        out_shape += [jax.ShapeDtypeStruct((B, QB, rows_b), F32)] * 2
        out_specs += [pl.BlockSpec((1, KA, rows_a), lambda b, t: (b, 0, 0))] * 2
        out_specs += [pl.BlockSpec((1, QB, tm), lambda b, t: (b, 0, jnp.maximum(t - first_cache_tile, 0)))] * 2
        scratch = [pltpu.VMEM((3 * GROUPS_B, tm, LANES), F32)] * 2
    kern = functools.partial(_proj_kernel, tm=tm, spread=spread, cache_rows_a=rows_a,
                             first_cache_tile=first_cache_tile, last_tile=nt - 1)
    return pl.pallas_call(
        kern,
        grid=(B, nt),
        in_specs=[pl.BlockSpec((1, tm, D), lambda b, t: (b, t, 0)),
                  _resident((D, IN_WIDTH)),
                  pl.BlockSpec((tm, LANES), lambda b, t: (t, 0)),
                  pl.BlockSpec((tm, LANES), lambda b, t: (t, 0))],
        out_specs=out_specs,
        out_shape=out_shape,
        scratch_shapes=scratch,
        compiler_params=_cparams(("arbitrary", "arbitrary")),
        name="proj_spread" if spread else "proj",
    )(x, w_in_bf, cos, sin)


def _band_attn_kernel(*refs, n_kgroups, q_per_k, inclusive, has_sink, emit_lse, n_sub):
    it = iter(refs)
    q_ref, kp_ref, kc_ref, vp_ref, vc_ref = (next(it) for _ in range(5))
    sink_ref = next(it) if has_sink else None
    o_ref = next(it)
    lse_ref = next(it) if emit_lse else None

    mt = pl.program_id(2)
    row = lax.broadcasted_iota(jnp.int32, (2 * BLK, 2 * BLK), 0)
    col = lax.broadcasted_iota(jnp.int32, (2 * BLK, 2 * BLK), 1)
    dist = (row % BLK) + BLK - col
    in_band = (dist >= 0) & ((dist <= BLK) if inclusive else (dist < BLK))
    first_valid = in_band & ((col >= BLK) | (mt > 0))
    lane = lax.broadcasted_iota(jnp.int32, (BLK, LANES), 1)
    lane_lo = lane < HEAD_DIM
    upper_rows = lax.broadcasted_iota(jnp.int32, (2 * BLK, 1), 0) >= BLK
    ones = jnp.ones((2 * BLK, LANES), BF16)
    zero_q = jnp.zeros((BLK, LANES), BF16)
    lse_c = [jnp.zeros((BLK, LANES), F32)] * n_sub

    for n in range(n_kgroups):
        ksl = slice(n * LANES, (n + 1) * LANES)
        kblk = [kp_ref[0, :, ksl]] + [kc_ref[0, i * BLK:(i + 1) * BLK, ksl] for i in range(n_sub)]
        vblk = [vp_ref[0, :, ksl]] + [vc_ref[0, i * BLK:(i + 1) * BLK, ksl] for i in range(n_sub)]
        for sub in range(n_sub):
            rows = slice(sub * BLK, (sub + 1) * BLK)
            valid = first_valid if sub == 0 else in_band
            k2 = jnp.concatenate([kblk[sub], kblk[sub + 1]], axis=0)
            v2e = jnp.concatenate([jnp.concatenate([vblk[sub], vblk[sub + 1]], axis=0), ones], axis=1)
            for j in range(q_per_k):
                g = n * q_per_k + j
                gsl = slice(g * LANES, (g + 1) * LANES)
                qg = q_ref[0, rows, gsl]
                qs = jnp.concatenate([jnp.where(lane_lo, qg, zero_q), jnp.where(lane_lo, zero_q, qg)], axis=0)
                s = lax.dot_general(qs, k2, NT_DIMS, preferred_element_type=F32)
                s = jnp.where(valid, s, NEG_INF)
                m = jnp.max(s, axis=1, keepdims=True)
                if has_sink:
                    lo_head = 2 * GROUP_A * n + j
                    sink = jnp.where(upper_rows, sink_ref[lo_head + GROUP_A], sink_ref[lo_head])
                    m = jnp.maximum(m, sink)
                p = jnp.exp(s - m)
                r = jnp.dot(p.astype(BF16), v2e, preferred_element_type=F32)
                l = r[:, LANES:]
                if has_sink:
                    l = l + jnp.exp(sink - m)
                on = r[:, :LANES] / l
                o_ref[0, rows, gsl] = jnp.where(lane_lo, on[:BLK], on[BLK:]).astype(o_ref.dtype)
                if emit_lse:
                    lse_rows = m + jnp.log(l)
                    lse = jnp.where(lane_lo, lse_rows[:BLK], lse_rows[BLK:])
                    lse_c[sub] = jnp.where((lane == g) | (lane == HEAD_DIM + g), lse, lse_c[sub])
    if emit_lse:
        for sub in range(n_sub):
            lse_ref[0, sub * BLK:(sub + 1) * BLK, :] = lse_c[sub]


def _band_attention(q, k, v, *, stride, inclusive, sinks=None, emit_lse=False, name="band_attn"):
    B, M, _ = q.shape
    Wq, Wk = q.shape[-1] // stride, k.shape[-1] // stride
    assert M % BLK == 0
    n_sub = max(d for d in (1, 2, 4) if (M // BLK) % d == 0)
    tq = n_sub * BLK
    cur = lambda w: pl.BlockSpec((1, tq, w), lambda b, r, m: (b, m, r))
    prv = lambda w: pl.BlockSpec((1, BLK, w), lambda b, r, m: (b, jnp.maximum(m * n_sub - 1, 0), r))
    args = [q, k, k, v, v]
    in_specs = [cur(Wq), prv(Wk), cur(Wk), prv(Wk), cur(Wk)]
    if sinks is not None:
        args.append(sinks)
        in_specs.append(pl.BlockSpec(memory_space=pltpu.SMEM))
    out_shape = [jax.ShapeDtypeStruct(q.shape, BF16)]
    out_specs = [cur(Wq)]
    if emit_lse:
        out_shape.append(jax.ShapeDtypeStruct((B, M, stride * LANES), F32))
        out_specs.append(cur(LANES))
    kern = functools.partial(_band_attn_kernel, n_kgroups=Wk // LANES, q_per_k=Wq // Wk, inclusive=inclusive,
                             has_sink=sinks is not None, emit_lse=emit_lse, n_sub=n_sub)
    outs = pl.pallas_call(
        kern,
        grid=(B, stride, M // tq),
        in_specs=in_specs,
        out_specs=out_specs,
        out_shape=out_shape,
        compiler_params=_cparams(("arbitrary",) * 3),
        name=name,
    )(*args)
    return outs if emit_lse else outs[0]


def _sample_masks(T, n_past_a, n_past_b):
    t = np.arange(T)[:, None]
    dist_c = n_past_a + t - np.arange(n_past_a)[None, :]
    dist_n = t - np.arange(LANES)[None, :]
    a_c = ((dist_c >= 0) & (dist_c < WIN_A)).astype(np.float32)
    a_n = ((dist_n >= 0) & (dist_n < WIN_A)).astype(np.float32)
    dist_c = n_past_b + t - np.arange(n_past_b)[None, :]
    b_c = np.zeros((T, n_past_b), np.float32)
    b_n = np.zeros((T, LANES), np.float32)
    for window, r in DIL_BRANCHES:
        b_c += ((dist_c >= 0) & (dist_c % r == 0) & (dist_c // r <= window // r)).astype(np.float32)
        b_n += ((dist_n >= 0) & (dist_n % r == 0) & (dist_n // r <= window // r)).astype(np.float32)
    return a_c, a_n, b_c, b_n


def _pad_rows(x, top):
    z = jnp.zeros((LANES - x.shape[0], x.shape[1]), F32)
    return jnp.concatenate([x, z] if top else [z, x], axis=0)


def _rolled(cache, new_t, T):
    n = cache.shape[1]
    out = pltpu.roll(cache, n - T, 1)
    keep = lax.broadcasted_iota(jnp.int32, (cache.shape[0], LANES), 1) < LANES - T
    return out, jnp.where(keep, out[:, n - LANES:], new_t)


def _softmax_parts(s_c, s_n, m_c, m_n, sink=None):
    s_c = jnp.where(m_c > 0, s_c, NEG_INF)
    s_n = jnp.where(m_n > 0, s_n, NEG_INF)
    m = jnp.maximum(jnp.max(s_c, axis=1, keepdims=True), jnp.max(s_n, axis=1, keepdims=True))
    if sink is not None:
        m = jnp.maximum(m, sink)
    p_c = m_c * jnp.exp(s_c - m)
    p_n = m_n * jnp.exp(s_n - m)
    l = jnp.sum(p_c, axis=1, keepdims=True) + jnp.sum(p_n, axis=1, keepdims=True)
    if sink is not None:
        l = l + jnp.exp(sink - m)
    return p_c, p_n, l


def _sample_sink_kernel(qa_ref, kan_ref, van_ref, cak_ref, cav_ref, sink_ref, mac_ref, man_ref,
                        a_ref, oak_ref, oav_ref, *, T):
    n_past = cak_ref.shape[2]
    for cache_ref, new_ref, out_ref in ((cak_ref, kan_ref, oak_ref), (cav_ref, van_ref, oav_ref)):
        out, last = _rolled(cache_ref[0], _pad_rows(new_ref[0], top=False).T, T)
        if n_past > LANES:
            out_ref[0, :, :n_past - LANES] = out[:, :n_past - LANES]
        out_ref[0, :, n_past - LANES:] = last

    lo = lax.broadcasted_iota(jnp.int32, (T, LANES), 1) < HEAD_DIM
    for p in range(N_KV_A // 2):
        ksl = slice(p * LANES, (p + 1) * LANES)
        blocks = []
        for j in range(GROUP_A):
            qg = qa_ref[0, :, (p * GROUP_A + j) * LANES:(p * GROUP_A + j + 1) * LANES]
            blocks += [jnp.where(lo, qg, 0.0), jnp.where(lo, 0.0, qg)]
        qs = jnp.concatenate(blocks, axis=0).astype(BF16)
        s_c = jnp.dot(qs, cak_ref[0, ksl, :].astype(BF16), preferred_element_type=F32)
        s_n = lax.dot_general(qs, _pad_rows(kan_ref[0, :, ksl], top=True).astype(BF16), NT_DIMS,
                              preferred_element_type=F32)
        p_c, p_n, l = _softmax_parts(s_c, s_n, mac_ref[...], man_ref[...], sink_ref[p][:, 0:1])
        r = lax.dot_general(p_c.astype(BF16), cav_ref[0, ksl, :].astype(BF16), NT_DIMS, preferred_element_type=F32)
        r = r + jnp.dot(p_n.astype(BF16), _pad_rows(van_ref[0, :, ksl], top=True).astype(BF16),
                        preferred_element_type=F32)
        r = r / l
        for j in range(GROUP_A):
            g = p * GROUP_A + j
            a_ref[0, :, g * LANES:(g + 1) * LANES] = jnp.where(lo, r[(2 * j) * T:(2 * j + 1) * T],
                                                               r[(2 * j + 1) * T:(2 * j + 2) * T])


def _sample_sink_attention(qa, kan, van, cak, cav, sinks):
    DB, T, _ = qa.shape
    n_past = cak.shape[2]
    assert T % 8 == 0 and T <= LANES and n_past % LANES == 0
    a_c, a_n, _, _ = _sample_masks(T, n_past, LANES)
    masks = [jnp.asarray(np.tile(m, (2 * GROUP_A, 1))) for m in (a_c, a_n)]
    heads = np.asarray([[GROUP_A * (2 * p + half) + j for j in range(GROUP_A) for half in range(2) for _ in range(T)]
                        for p in range(N_KV_A // 2)], np.int32)
    sink_rows = jnp.broadcast_to(sinks[heads][:, :, None], heads.shape + (LANES,)).astype(F32)
    per_b = lambda r, w: pl.BlockSpec((1, r, w), lambda b: (b, 0, 0))
    return pl.pallas_call(
        functools.partial(_sample_sink_kernel, T=T),
        grid=(DB,),
        in_specs=[per_b(T, QA), per_b(T, KA), per_b(T, KA), per_b(KA, n_past), per_b(KA, n_past),
                  _resident(sink_rows.shape), _resident(masks[0].shape), _resident(masks[1].shape)],
        out_specs=[per_b(T, QA), per_b(KA, n_past), per_b(KA, n_past)],
        out_shape=[jax.ShapeDtypeStruct((DB, T, QA), F32), jax.ShapeDtypeStruct(cak.shape, F32),
                   jax.ShapeDtypeStruct(cav.shape, F32)],
        compiler_params=_cparams(("arbitrary",)),
        name="sample_sink_attn",
    )(qa, kan, van, cak, cav, sink_rows, *masks)


def _sample_dilated_step(qb_ref, kbn_ref, vbn_ref, cbk_ref, cbv_ref, mbc_ref, mbn_ref, b_ref, obk_ref, obv_ref, T):
    width = qb_ref.shape[2]
    n_past = cbk_ref.shape[2]
    n_heads = width // HEAD_DIM
    rows_b = n_heads * T
    row_head = lax.broadcasted_iota(jnp.int32, (rows_b, width), 0) // T
    lane_head = lax.broadcasted_iota(jnp.int32, (rows_b, width), 1) // HEAD_DIM
    diag = row_head == lane_head
    qblk = jnp.where(diag, jnp.concatenate([qb_ref[0]] * n_heads, axis=0), 0.0).astype(BF16)
    s_c = jnp.dot(qblk, cbk_ref[0].astype(BF16), preferred_element_type=F32)
    s_n = lax.dot_general(qblk, _pad_rows(kbn_ref[0], top=True).astype(BF16), NT_DIMS, preferred_element_type=F32)
    p_c, p_n, l = _softmax_parts(s_c, s_n, mbc_ref[...], mbn_ref[...])
    r = lax.dot_general(p_c.astype(BF16), cbv_ref[0].astype(BF16), NT_DIMS, preferred_element_type=F32)
    r = r + jnp.dot(p_n.astype(BF16), _pad_rows(vbn_ref[0], top=True).astype(BF16), preferred_element_type=F32)
    r = jnp.where(diag, r / l, 0.0)
    out = r[0:T]
    for h in range(1, n_heads):
        out = out + r[h * T:(h + 1) * T]
    b_ref[0] = out

    for cache_ref, new_ref, out_ref in ((cbk_ref, kbn_ref, obk_ref), (cbv_ref, vbn_ref, obv_ref)):
        new_t = _pad_rows(new_ref[0], top=False).T
        for h in range(n_heads):
            rows = slice(h * HEAD_DIM, (h + 1) * HEAD_DIM)
            out, last = _rolled(cache_ref[0, rows, :], new_t[rows], T)
            out_ref[0, rows, :n_past - LANES] = out[:, :n_past - LANES]
            out_ref[0, rows, n_past - LANES:] = last


def _layer_norm(z, g, b):
    mu = jnp.mean(z, axis=-1, keepdims=True)
    zc = z - mu
    var = jnp.mean(zc * zc, axis=-1, keepdims=True)
    return zc * lax.rsqrt(var + LN_EPS) * g + b


def _outproj_kernel(x_ref, a_ref, b_ref, wa_ref, wb_ref, g_ref, be_ref, h_ref):
    proj = jnp.dot(a_ref[...].astype(BF16), wa_ref[...], preferred_element_type=F32)
    proj = proj + jnp.dot(b_ref[...].astype(BF16), wb_ref[...], preferred_element_type=F32)
    h_ref[...] = _layer_norm(DN_ALPHA * x_ref[...] + proj, g_ref[...], be_ref[...])


def _outproj_ln(x, a, b, w_a, w_b, g, be, *, tm):
    N, D = x.shape
    assert N % tm == 0
    rows = lambda w: pl.BlockSpec((tm, w), lambda i: (i, 0))
    return pl.pallas_call(
        _outproj_kernel,
        grid=(N // tm,),
        in_specs=[rows(D), rows(a.shape[1]), rows(b.shape[1]), _resident(w_a.shape), _resident(w_b.shape),
                  _resident(g.shape), _resident(be.shape)],
        out_specs=rows(D),
        out_shape=jax.ShapeDtypeStruct((N, D), F32),
        compiler_params=_cparams(("arbitrary",)),
        name="outproj_ln",
    )(x, a, b, w_a, w_b, g, be)


def _merge_outproj_kernel(x_ref, a_ref, o1_ref, l1_ref, o4_ref, l4_ref, o16_ref, l16_ref, e_ref,
                          wa_ref, wb_ref, g_ref, be_ref, h_ref, n4_ref, n16_ref, nl_ref, *, tm):
    m4, m16 = tm // 4, tm // 16
    for r in range(4):
        nl_ref[0, pl.ds(r, m4, stride=4), :] = l4_ref[:, r * LANES:(r + 1) * LANES]
        for g in range(GROUPS_B):
            n4_ref[g, pl.ds(r, m4, stride=4), :] = o4_ref[:, r * QB + g * LANES:r * QB + (g + 1) * LANES].astype(F32)
    for r in range(16):
        nl_ref[1, pl.ds(r, m16, stride=16), :] = l16_ref[:, r * LANES:(r + 1) * LANES]
        for g in range(GROUPS_B):
            n16_ref[g, pl.ds(r, m16, stride=16), :] = (
                o16_ref[:, r * QB + g * LANES:r * QB + (g + 1) * LANES].astype(F32))

    lses = [l1_ref[...], nl_ref[0], nl_ref[1]]
    mx = jnp.maximum(jnp.maximum(lses[0], lses[1]), lses[2])
    ws = [jnp.exp(l - mx) for l in lses]
    inv = 1.0 / (ws[0] + ws[1] + ws[2])
    e = e_ref[...]

    def widen(w):
        hi = w.astype(BF16)
        lo = (w - hi.astype(F32)).astype(BF16)
        return jnp.dot(hi, e, preferred_element_type=F32) + jnp.dot(lo, e, preferred_element_type=F32)

    ws = [widen(w * inv) for w in ws]
    merged = []
    for g in range(GROUPS_B):
        sl = slice(g * LANES, (g + 1) * LANES)
        bg = ws[0][:, sl] * o1_ref[:, sl].astype(F32) + ws[1][:, sl] * n4_ref[g] + ws[2][:, sl] * n16_ref[g]
        merged.append(bg.astype(BF16))
    proj = jnp.dot(a_ref[...], wa_ref[...], preferred_element_type=F32)
    proj = proj + jnp.dot(jnp.concatenate(merged, axis=1), wb_ref[...], preferred_element_type=F32)
    h_ref[...] = _layer_norm(DN_ALPHA * x_ref[...] + proj, g_ref[...], be_ref[...])


def _merge_outproj_ln(x, a, o1, l1, o4, l4, o16, l16, w_a, w_b, g, be, *, tm):
    N, D = x.shape
    assert N % tm == 0 and tm % 256 == 0
    rows = lambda n, w: pl.BlockSpec((n, w), lambda i: (i, 0))
    expander = jnp.asarray(_lse_expander(), BF16)
    return pl.pallas_call(
        functools.partial(_merge_outproj_kernel, tm=tm),
        grid=(N // tm,),
        in_specs=[rows(tm, D), rows(tm, QA), rows(tm, QB), rows(tm, LANES),
                  rows(tm // 4, 4 * QB), rows(tm // 4, 4 * LANES), rows(tm // 16, 16 * QB), rows(tm // 16, 16 * LANES),
                  _resident(expander.shape), _resident(w_a.shape), _resident(w_b.shape),
                  _resident(g.shape), _resident(be.shape)],
        out_specs=rows(tm, D),
        out_shape=jax.ShapeDtypeStruct((N, D), F32),
        scratch_shapes=[pltpu.VMEM((GROUPS_B, tm, LANES), F32), pltpu.VMEM((GROUPS_B, tm, LANES), F32),
                        pltpu.VMEM((2, tm, LANES), F32)],
        compiler_params=_cparams(("arbitrary",)),
        name="merge_outproj_ln",
    )(x, a, o1, l1, o4, l4, o16, l16, expander, w_a, w_b, g, be)


def _ffn_kernel(*refs, tm, n_chunks, n_units, guard_units, T):
    h_hbm, wu_ref, wd_ref, g_ref, be_ref = refs[:5]
    if n_units:
        sample_in = refs[5:12]
        y_ref, b_ref, obk_ref, obv_ref, hb_ref, sem = refs[12:]
    else:
        y_ref, hb_ref, sem = refs[5:]
    i = pl.program_id(0)
    j = pl.program_id(1)

    @pl.when(j == 0)
    def _():
        copy = pltpu.make_async_copy(h_hbm.at[pl.ds(pl.multiple_of(i * tm, tm), tm), :], y_ref, sem.at[0])
        copy.start()
        copy.wait()
        h = y_ref[...]
        hb_ref[...] = h.astype(BF16)
        y_ref[...] = DN_ALPHA * h

    u = jnp.dot(hb_ref[...], wu_ref[...], preferred_element_type=F32)
    u = jnp.maximum(u, 0.0)
    y_ref[...] += jnp.dot((u * u).astype(BF16), wd_ref[...], preferred_element_type=F32)

    if n_units:
        step_args = sample_in + (b_ref, obk_ref, obv_ref, T)
        if guard_units:
            pl.when(i * n_chunks + j < n_units)(lambda: _sample_dilated_step(*step_args))
        else:
            _sample_dilated_step(*step_args)

    @pl.when(j == n_chunks - 1)
    def _():
        y_ref[...] = _layer_norm(y_ref[...], g_ref[...], be_ref[...])


SAMPLE_UNIT_WIDTH = 4 * HEAD_DIM


def _ffn_ln(h, w_up, w_down, g, be, *, tm, tf, sample=None):
    N, D = h.shape
    F = w_up.shape[1]
    assert N % tm == 0 and F % tf == 0
    n_tiles, n_chunks = N // tm, F // tf
    in_specs = [pl.BlockSpec(memory_space=pl.ANY),
                pl.BlockSpec((D, tf), lambda i, j: (0, j)),
                pl.BlockSpec((tf, D), lambda i, j: (j, 0)),
                _resident(g.shape), _resident(be.shape)]
    args = [h, w_up, w_down, g, be]
    out_specs = [pl.BlockSpec((tm, D), lambda i, j: (i, 0))]
    out_shape = [jax.ShapeDtypeStruct((N, D), F32)]
    n_units, T = 0, 0
    if sample is not None:
        qb, kbn, vbn, cbk, cbv = sample
        DB, T, _ = qb.shape
        n_past = cbk.shape[2]
        w = SAMPLE_UNIT_WIDTH
        parts = QB // w
        n_units = DB * parts
        assert n_units <= n_tiles * n_chunks and T % 8 == 0 and T <= LANES and n_past % LANES == 0 and n_past > LANES
        _, _, b_c, b_n = _sample_masks(T, LANES, n_past)
        masks = [jnp.asarray(np.tile(m, (w // HEAD_DIM, 1))) for m in (b_c, b_n)]

        def unit(i, j):
            u = jnp.minimum(i * n_chunks + j, n_units - 1)
            return u // parts, u % parts

        new = pl.BlockSpec((1, T, w), lambda i, j: (unit(i, j)[0], 0, unit(i, j)[1]))
        cache = pl.BlockSpec((1, w, n_past), lambda i, j: (unit(i, j)[0], unit(i, j)[1], 0))
        in_specs += [new, new, new, cache, cache, _resident(masks[0].shape), _resident(masks[1].shape)]
        args += [qb, kbn, vbn, cbk, cbv] + masks
        out_specs += [new, cache, cache]
        out_shape += [jax.ShapeDtypeStruct(qb.shape, F32), jax.ShapeDtypeStruct(cbk.shape, F32),
                      jax.ShapeDtypeStruct(cbv.shape, F32)]
    outs = pl.pallas_call(
        functools.partial(_ffn_kernel, tm=tm, n_chunks=n_chunks, n_units=n_units,
                          guard_units=n_units < n_tiles * n_chunks, T=T),
        grid=(n_tiles, n_chunks),
        in_specs=in_specs,
        out_specs=out_specs,
        out_shape=out_shape,
        scratch_shapes=[pltpu.VMEM((tm, D), BF16), pltpu.SemaphoreType.DMA((1,))],
        compiler_params=_cparams(("arbitrary", "arbitrary")),
        name="ffn_ln_sample" if sample is not None else "ffn_ln",
    )(*args)
    return outs if sample is not None else outs[0]


def _token_tile(n, cap):
    t = min(n, cap)
    while n % t:
        t //= 2
    return t


def _position_minor(cache):
    _, DB, n, H, Dh = cache.shape
    return jnp.transpose(cache[0], (0, 2, 3, 1)).reshape(DB, H * Dh, n)


def _position_major(x, heads):
    DB, _, n = x.shape
    return jnp.transpose(x.reshape(DB, heads, HEAD_DIM, n), (0, 3, 1, 2))[None]


def kernel(x_prompt, x_sample, cache_a_k, cache_a_v, cache_b_k, cache_b_v,
           w_in, sinks, w_out, ln1_g, ln1_b, w_up, w_down, ln2_g, ln2_b):
    B, L, D = x_prompt.shape
    DB, T, _ = x_sample.shape
    assert w_in.shape[0] == 1, "one layer"
    assert DIL_BRANCHES == ((BLK, 1), (4 * BLK, 4), (16 * BLK, 16))
    perm = _qa_head_perm()

    w_in_bf = jnp.concatenate([w_in[0][:, perm], w_in[0][:, QA:]], axis=1).astype(BF16)
    w_out_a = w_out[0][:QA][perm].astype(BF16)
    w_out_b = w_out[0][QA:].astype(BF16)
    w_up_bf = w_up[0].astype(BF16)
    w_down_bf = w_down[0].astype(BF16)
    g1, b1, g2, b2 = ln1_g[0][None], ln1_b[0][None], ln2_g[0][None], ln2_b[0][None]

    cos_p, sin_p = _rope_tables(jnp.arange(L))
    (qa, ka, va, qb1, kb1, vb1, qb4, kb4, vb4, qb16, kb16, vb16, pak, pav, pbk, pbv) = _project(
        x_prompt, w_in_bf, cos_p, sin_p, tm=_token_tile(L, 256), out_dtype=BF16, spread=True)
    a_p = _band_attention(qa, ka, va, stride=1, inclusive=False, sinks=sinks[0], name="sink_attn")
    o1, l1 = _band_attention(qb1, kb1, vb1, stride=1, inclusive=True, emit_lse=True, name="dilated_r1")
    o4, l4 = _band_attention(qb4, kb4, vb4, stride=4, inclusive=True, emit_lse=True, name="dilated_r4")
    o16, l16 = _band_attention(qb16, kb16, vb16, stride=16, inclusive=True, emit_lse=True, name="dilated_r16")
    N_p = B * L
    flat = lambda z: z.reshape(-1, z.shape[-1])
    h_p = _merge_outproj_ln(flat(x_prompt), flat(a_p), flat(o1), flat(l1), flat(o4), flat(l4), flat(o16), flat(l16),
                            w_out_a, w_out_b, g1, b1, tm=_token_tile(L, 256))

    N_s = DB * T
    cos_s, sin_s = _rope_tables(PAST_LEN + jnp.arange(T))
    cos_s = jnp.tile(cos_s, (DB, 1))
    sin_s = jnp.tile(sin_s, (DB, 1))
    sq = _project(x_sample.reshape(1, N_s, D), w_in_bf, cos_s, sin_s, tm=_token_tile(N_s, 256),
                  out_dtype=F32, spread=False)
    qa_s, kan, van, qb_s, kbn, vbn = [z.reshape(DB, T, z.shape[-1]) for z in sq]
    a_s, sak, sav = _sample_sink_attention(qa_s, kan, van, _position_minor(cache_a_k), _position_minor(cache_a_v),
                                           sinks[0])
    y_p, b_s, sbk, sbv = _ffn_ln(h_p, w_up_bf, w_down_bf, g2, b2, tm=_token_tile(N_p, 1024), tf=256,
                                 sample=(qb_s, kbn, vbn, _position_minor(cache_b_k), _position_minor(cache_b_v)))
    h_s = _outproj_ln(x_sample.reshape(N_s, D), a_s.reshape(N_s, QA), b_s.reshape(N_s, QB),
                      w_out_a, w_out_b, g1, b1, tm=_token_tile(N_s, 512))
    y_s = _ffn_ln(h_s, w_up_bf, w_down_bf, g2, b2, tm=_token_tile(N_s, 1024), tf=512)

    return (y_p.reshape(B, L, D), y_s.reshape(DB, T, D),
            _position_major(pak, N_KV_A), _position_major(pav, N_KV_A),
            _position_major(pbk, N_HEADS_B), _position_major(pbv, N_HEADS_B),
            _position_major(sak, N_KV_A), _position_major(sav, N_KV_A),
            _position_major(sbk, N_HEADS_B), _position_major(sbv, N_HEADS_B))
```

```python
import functools

import numpy as np
import jax
import jax.numpy as jnp
from jax import lax
from jax.experimental import pallas as pl
from jax.experimental.pallas import tpu as pltpu

F32 = jnp.float32
BF16 = jnp.bfloat16

HEAD_DIM = 64
LANES = 128
N_HEADS_A = 16
N_KV_A = 4
GROUP_A = N_HEADS_A // N_KV_A
N_HEADS_B = 16
WIN_A = 128
DIL_BRANCHES = ((128, 1), (512, 4), (2048, 16))
WIN_B = 2048
BLK = 128
ROPE_THETA = 10000.0
LN_EPS = 1e-5
DN_ALPHA = 2.0 ** 0.25
SCALE = HEAD_DIM ** -0.5
NEG_INF = -1e30
PAST_LEN = 16384
QA = N_HEADS_A * HEAD_DIM
KA = N_KV_A * HEAD_DIM
QB = N_HEADS_B * HEAD_DIM
IN_WIDTH = QA + 2 * KA + 3 * QB
GROUPS_B = QB // LANES
VMEM_LIMIT = 56 * 1024 * 1024
NT_DIMS = (((1,), (1,)), ((), ()))


def _cparams(sem):
    return pltpu.CompilerParams(dimension_semantics=sem, vmem_limit_bytes=VMEM_LIMIT)


def _resident(shape):
    nd = len(shape)
    return pl.BlockSpec(shape, lambda *_: (0,) * nd, pipeline_mode=pl.Buffered(1))


def _qa_head_perm():
    cols = []
    for p in range(N_KV_A // 2):
        for j in range(GROUP_A):
            for half in range(2):
                head = GROUP_A * (2 * p + half) + j
                cols.extend(range(head * HEAD_DIM, (head + 1) * HEAD_DIM))
    return np.asarray(cols, np.int32)


def _rope_tables(pos):
    half = HEAD_DIM // 2
    inv = 1.0 / (ROPE_THETA ** (jnp.arange(half, dtype=F32) / half))
    ang = pos.astype(F32)[:, None] * inv[None, :]
    cos = jnp.cos(ang)
    sin = jnp.sin(ang)
    return (jnp.concatenate([cos, cos, cos, cos], -1),
            jnp.concatenate([-sin, sin, -sin, sin], -1))


def _lse_expander():
    e = np.zeros((LANES, QB), np.float32)
    for n in range(GROUPS_B):
        e[n, n * LANES:n * LANES + HEAD_DIM] = 1.0
        e[HEAD_DIM + n, n * LANES + HEAD_DIM:(n + 1) * LANES] = 1.0
    return e


def _proj_kernel(x_ref, w_ref, cos_ref, sin_ref, *refs, tm, spread, cache_rows_a):
    qa_ref, ka_ref, va_ref = refs[:3]
    nat_b = refs[3:6]
    if spread:
        by4, by16, tails, (s1_ref, s2_ref) = refs[6:9], refs[9:12], refs[12:16], refs[16:18]
    xb = x_ref[0].astype(BF16)
    cos = cos_ref[...]
    sin = sin_ref[...]
    lane = lax.broadcasted_iota(jnp.int32, (tm, LANES), 1)
    lo_half = (lane % HEAD_DIM) < (HEAD_DIM // 2)

    def rope(hg):
        swapped = jnp.where(lo_half, pltpu.roll(hg, LANES - HEAD_DIM // 2, 1), pltpu.roll(hg, HEAD_DIM // 2, 1))
        return hg * cos + swapped * sin

    def groups(col0, n):
        h = jnp.dot(xb, w_ref[:, col0:col0 + n * LANES], preferred_element_type=F32)
        return [h[:, g * LANES:(g + 1) * LANES] for g in range(n)]

    def put(ref, g, val):
        ref[0, :, g * LANES:(g + 1) * LANES] = val.astype(ref.dtype)

    def put_spread(which, g, val):
        slab = which * GROUPS_B + g
        n4, n16 = tm // 4, tm // 16
        s1_ref[slab] = val
        for r4 in range(4):
            part = s1_ref[slab, pl.ds(r4, n4, stride=4), :]
            by4[which][0, :, r4 * QB + g * LANES:r4 * QB + (g + 1) * LANES] = part.astype(BF16)
            s2_ref[slab, r4 * n4:(r4 + 1) * n4, :] = part
        for r4 in range(4):
            for a in range(4):
                part = s2_ref[slab, pl.ds(r4 * n4 + a, n16, stride=4), :]
                r16 = 4 * a + r4
                by16[which][0, :, r16 * QB + g * LANES:r16 * QB + (g + 1) * LANES] = part.astype(BF16)

    for c in range(QA // 512):
        for g, hg in enumerate(groups(c * 512, 4)):
            put(qa_ref, c * 4 + g, rope(hg) * SCALE)

    kva = groups(QA, 4)
    ka_g = [rope(kva[0]), rope(kva[1])]
    va_g = [kva[2], kva[3]]
    for g in range(2):
        put(ka_ref, g, ka_g[g])
        put(va_ref, g, va_g[g])
    if spread:
        for g in range(2):
            tails[0][0, g * LANES:(g + 1) * LANES, :] = ka_g[g][tm - cache_rows_a:, :].T
            tails[1][0, g * LANES:(g + 1) * LANES, :] = va_g[g][tm - cache_rows_a:, :].T

    base = QA + 2 * KA
    for which in range(3):
        for c in range(QB // 512):
            vals = groups(base + QB * which + c * 512, 4)
            if which < 2:
                vals = [rope(v) for v in vals]
            if which == 0:
                vals = [v * SCALE for v in vals]
            for g, v in enumerate(vals):
                put(nat_b[which], c * 4 + g, v)
                if spread:
                    put_spread(which, c * 4 + g, v)
            if spread and which > 0:
                for g, v in enumerate(vals):
                    tails[1 + which][0, (c * 4 + g) * LANES:(c * 4 + g + 1) * LANES, :] = v.T


def _project(x, w_in_bf, cos, sin, *, tm, out_dtype, spread):
    B, L, D = x.shape
    nt = L // tm
    assert L % tm == 0
    widths = (QA, KA, KA, QB, QB, QB)
    out_shape = [jax.ShapeDtypeStruct((B, L, w), out_dtype) for w in widths]
    out_specs = [pl.BlockSpec((1, tm, w), lambda b, t: (b, t, 0)) for w in widths]
    scratch = []
    rows_a = min(WIN_A, L)
    rows_b = min(WIN_B, L)
    first_cache_tile = (L - rows_b) // tm
    if spread:
        assert tm % 256 == 0 and rows_a <= tm and (L - rows_b) % tm == 0
        for r in (4, 16):
            out_shape += [jax.ShapeDtypeStruct((B, L // r, r * QB), BF16)] * 3
            out_specs += [pl.BlockSpec((1, tm // r, r * QB), lambda b, t: (b, t, 0))] * 3
        out_shape += [jax.ShapeDtypeStruct((B, KA, rows_a), F32)] * 2
        out_shape += [jax.ShapeDtypeStruct((B, QB, rows_b), F32)] * 2
        out_specs += [pl.BlockSpec((1, KA, rows_a), lambda b, t: (b, 0, 0))] * 2
        out_specs += [pl.BlockSpec((1, QB, tm), lambda b, t: (b, 0, jnp.maximum(t - first_cache_tile, 0)))] * 2
        scratch = [pltpu.VMEM((3 * GROUPS_B, tm, LANES), F32)] * 2
    kern = functools.partial(_proj_kernel, tm=tm, spread=spread, cache_rows_a=rows_a)
    return pl.pallas_call(
        kern,
        grid=(B, nt),
        in_specs=[pl.BlockSpec((1, tm, D), lambda b, t: (b, t, 0)),
                  _resident((D, IN_WIDTH)),
                  pl.BlockSpec((tm, LANES), lambda b, t: (t, 0)),
                  pl.BlockSpec((tm, LANES), lambda b, t: (t, 0))],
        out_specs=out_specs,
        out_shape=out_shape,
        scratch_shapes=scratch,
        compiler_params=_cparams(("arbitrary", "arbitrary")),
        name="proj_spread" if spread else "proj",
    )(x, w_in_bf, cos, sin)


def _band_attn_kernel(*refs, n_kgroups, q_per_k, inclusive, has_sink, emit_lse, n_sub):
    it = iter(refs)
    q_ref, kp_ref, kc_ref, vp_ref, vc_ref = (next(it) for _ in range(5))
    sink_ref = next(it) if has_sink else None
    o_ref = next(it)
    lse_ref = next(it) if emit_lse else None

    mt = pl.program_id(2)
    row = lax.broadcasted_iota(jnp.int32, (2 * BLK, 2 * BLK), 0)
    col = lax.broadcasted_iota(jnp.int32, (2 * BLK, 2 * BLK), 1)
    dist = (row % BLK) + BLK - col
    in_band = (dist >= 0) & ((dist <= BLK) if inclusive else (dist < BLK))
    first_valid = in_band & ((col >= BLK) | (mt > 0))
    lane = lax.broadcasted_iota(jnp.int32, (BLK, LANES), 1)
    lane_lo = lane < HEAD_DIM
    upper_rows = lax.broadcasted_iota(jnp.int32, (2 * BLK, 1), 0) >= BLK
    ones = jnp.ones((2 * BLK, LANES), BF16)
    zero_q = jnp.zeros((BLK, LANES), BF16)
    lse_c = [jnp.zeros((BLK, LANES), F32)] * n_sub

    for n in range(n_kgroups):
        ksl = slice(n * LANES, (n + 1) * LANES)
        kblk = [kp_ref[0, :, ksl]] + [kc_ref[0, i * BLK:(i + 1) * BLK, ksl] for i in range(n_sub)]
        vblk = [vp_ref[0, :, ksl]] + [vc_ref[0, i * BLK:(i + 1) * BLK, ksl] for i in range(n_sub)]
        for sub in range(n_sub):
            rows = slice(sub * BLK, (sub + 1) * BLK)
            valid = first_valid if sub == 0 else in_band
            k2 = jnp.concatenate([kblk[sub], kblk[sub + 1]], axis=0)
            v2e = jnp.concatenate([jnp.concatenate([vblk[sub], vblk[sub + 1]], axis=0), ones], axis=1)
            for j in range(q_per_k):
                g = n * q_per_k + j
                gsl = slice(g * LANES, (g + 1) * LANES)
                qg = q_ref[0, rows, gsl]
                qs = jnp.concatenate([jnp.where(lane_lo, qg, zero_q), jnp.where(lane_lo, zero_q, qg)], axis=0)
                s = lax.dot_general(qs, k2, NT_DIMS, preferred_element_type=F32)
                s = jnp.where(valid, s, NEG_INF)
                m = jnp.max(s, axis=1, keepdims=True)
                if has_sink:
                    lo_head = 2 * GROUP_A * n + j
                    sink = jnp.where(upper_rows, sink_ref[lo_head + GROUP_A], sink_ref[lo_head])
                    m = jnp.maximum(m, sink)
                p = jnp.exp(s - m)
                r = jnp.dot(p.astype(BF16), v2e, preferred_element_type=F32)
                l = r[:, LANES:]
                if has_sink:
                    l = l + jnp.exp(sink - m)
                on = r[:, :LANES] / l
                o_ref[0, rows, gsl] = jnp.where(lane_lo, on[:BLK], on[BLK:]).astype(o_ref.dtype)
                if emit_lse:
                    lse_rows = m + jnp.log(l)
                    lse = jnp.where(lane_lo, lse_rows[:BLK], lse_rows[BLK:])
                    lse_c[sub] = jnp.where((lane == g) | (lane == HEAD_DIM + g), lse, lse_c[sub])
    if emit_lse:
        for sub in range(n_sub):
            lse_ref[0, sub * BLK:(sub + 1) * BLK, :] = lse_c[sub]


def _band_attention(q, k, v, *, stride, inclusive, sinks=None, emit_lse=False, name="band_attn"):
    B, M, _ = q.shape
    Wq, Wk = q.shape[-1] // stride, k.shape[-1] // stride
    assert M % BLK == 0
    n_sub = max(d for d in (1, 2, 4) if (M // BLK) % d == 0)
    tq = n_sub * BLK
    cur = lambda w: pl.BlockSpec((1, tq, w), lambda b, r, m: (b, m, r))
    prv = lambda w: pl.BlockSpec((1, BLK, w), lambda b, r, m: (b, jnp.maximum(m * n_sub - 1, 0), r))
    args = [q, k, k, v, v]
    in_specs = [cur(Wq), prv(Wk), cur(Wk), prv(Wk), cur(Wk)]
    if sinks is not None:
        args.append(sinks)
        in_specs.append(pl.BlockSpec(memory_space=pltpu.SMEM))
    out_shape = [jax.ShapeDtypeStruct(q.shape, BF16)]
    out_specs = [cur(Wq)]
    if emit_lse:
        out_shape.append(jax.ShapeDtypeStruct((B, M, stride * LANES), F32))
        out_specs.append(cur(LANES))
    kern = functools.partial(_band_attn_kernel, n_kgroups=Wk // LANES, q_per_k=Wq // Wk, inclusive=inclusive,
                             has_sink=sinks is not None, emit_lse=emit_lse, n_sub=n_sub)
    outs = pl.pallas_call(
        kern,
        grid=(B, stride, M // tq),
        in_specs=in_specs,
        out_specs=out_specs,
        out_shape=out_shape,
        compiler_params=_cparams(("arbitrary",) * 3),
        name=name,
    )(*args)
    return outs if emit_lse else outs[0]


def _sample_masks(T, n_past_a, n_past_b):
    t = np.arange(T)[:, None]
    dist_c = n_past_a + t - np.arange(n_past_a)[None, :]
    dist_n = t - np.arange(LANES)[None, :]
    a_c = ((dist_c >= 0) & (dist_c < WIN_A)).astype(np.float32)
    a_n = ((dist_n >= 0) & (dist_n < WIN_A)).astype(np.float32)
    dist_c = n_past_b + t - np.arange(n_past_b)[None, :]
    b_c = np.zeros((T, n_past_b), np.float32)
    b_n = np.zeros((T, LANES), np.float32)
    for window, r in DIL_BRANCHES:
        b_c += ((dist_c >= 0) & (dist_c % r == 0) & (dist_c // r <= window // r)).astype(np.float32)
        b_n += ((dist_n >= 0) & (dist_n % r == 0) & (dist_n // r <= window // r)).astype(np.float32)
    return a_c, a_n, b_c, b_n


def _pad_rows(x, top):
    z = jnp.zeros((LANES - x.shape[0], x.shape[1]), F32)
    return jnp.concatenate([x, z] if top else [z, x], axis=0)


def _rolled(cache, new_t, T):
    n = cache.shape[1]
    out = pltpu.roll(cache, n - T, 1)
    keep = lax.broadcasted_iota(jnp.int32, (cache.shape[0], LANES), 1) < LANES - T
    return out, jnp.where(keep, out[:, n - LANES:], new_t)


def _softmax_parts(s_c, s_n, m_c, m_n, sink=None):
    s_c = jnp.where(m_c > 0, s_c, NEG_INF)
    s_n = jnp.where(m_n > 0, s_n, NEG_INF)
    m = jnp.maximum(jnp.max(s_c, axis=1, keepdims=True), jnp.max(s_n, axis=1, keepdims=True))
    if sink is not None:
        m = jnp.maximum(m, sink)
    p_c = m_c * jnp.exp(s_c - m)
    p_n = m_n * jnp.exp(s_n - m)
    l = jnp.sum(p_c, axis=1, keepdims=True) + jnp.sum(p_n, axis=1, keepdims=True)
    if sink is not None:
        l = l + jnp.exp(sink - m)
    return p_c, p_n, l


def _sample_sink_kernel(*refs, T):
    for bi in range(refs[0].shape[0]):
        _sample_sink_one(bi, *refs, T)


def _sample_sink_one(bi, qa_ref, kan_ref, van_ref, cak_ref, cav_ref, sink_ref, mac_ref, man_ref,
                     a_ref, oak_ref, oav_ref, T):
    n_past = cak_ref.shape[2]
    for cache_ref, new_ref, out_ref in ((cak_ref, kan_ref, oak_ref), (cav_ref, van_ref, oav_ref)):
        out, last = _rolled(cache_ref[bi], _pad_rows(new_ref[bi], top=False).T, T)
        if n_past > LANES:
            out_ref[bi, :, :n_past - LANES] = out[:, :n_past - LANES]
        out_ref[bi, :, n_past - LANES:] = last

    lo = lax.broadcasted_iota(jnp.int32, (T, LANES), 1) < HEAD_DIM
    for p in range(N_KV_A // 2):
        ksl = slice(p * LANES, (p + 1) * LANES)
        blocks = []
        for j in range(GROUP_A):
            qg = qa_ref[bi, :, (p * GROUP_A + j) * LANES:(p * GROUP_A + j + 1) * LANES]
            blocks += [jnp.where(lo, qg, 0.0), jnp.where(lo, 0.0, qg)]
        qs = jnp.concatenate(blocks, axis=0).astype(BF16)
        s_c = jnp.dot(qs, cak_ref[bi, ksl, :].astype(BF16), preferred_element_type=F32)
        s_n = lax.dot_general(qs, _pad_rows(kan_ref[bi, :, ksl], top=True).astype(BF16), NT_DIMS,
                              preferred_element_type=F32)
        p_c, p_n, l = _softmax_parts(s_c, s_n, mac_ref[...], man_ref[...], sink_ref[p][:, 0:1])
        r = lax.dot_general(p_c.astype(BF16), cav_ref[bi, ksl, :].astype(BF16), NT_DIMS, preferred_element_type=F32)
        r = r + jnp.dot(p_n.astype(BF16), _pad_rows(van_ref[bi, :, ksl], top=True).astype(BF16),
                        preferred_element_type=F32)
        r = r / l
        for j in range(GROUP_A):
            g = p * GROUP_A + j
            a_ref[bi, :, g * LANES:(g + 1) * LANES] = jnp.where(lo, r[(2 * j) * T:(2 * j + 1) * T],
                                                               r[(2 * j + 1) * T:(2 * j + 2) * T])


def _sample_sink_attention(qa, kan, van, cak, cav, sinks):
    DB, T, _ = qa.shape
    n_past = cak.shape[2]
    assert T % 8 == 0 and T <= LANES and n_past % LANES == 0
    a_c, a_n, _, _ = _sample_masks(T, n_past, LANES)
    masks = [jnp.asarray(np.tile(m, (2 * GROUP_A, 1))) for m in (a_c, a_n)]
    heads = np.asarray([[GROUP_A * (2 * p + half) + j for j in range(GROUP_A) for half in range(2) for _ in range(T)]
                        for p in range(N_KV_A // 2)], np.int32)
    sink_rows = jnp.broadcast_to(sinks[heads][:, :, None], heads.shape + (LANES,)).astype(F32)
    nb = max(d for d in (1, 2, 4) if DB % d == 0)
    per_b = lambda r, w: pl.BlockSpec((nb, r, w), lambda b: (b, 0, 0))
    return pl.pallas_call(
        functools.partial(_sample_sink_kernel, T=T),
        grid=(DB // nb,),
        in_specs=[per_b(T, QA), per_b(T, KA), per_b(T, KA), per_b(KA, n_past), per_b(KA, n_past),
                  _resident(sink_rows.shape), _resident(masks[0].shape), _resident(masks[1].shape)],
        out_specs=[per_b(T, QA), per_b(KA, n_past), per_b(KA, n_past)],
        out_shape=[jax.ShapeDtypeStruct((DB, T, QA), F32), jax.ShapeDtypeStruct(cak.shape, F32),
                   jax.ShapeDtypeStruct(cav.shape, F32)],
        compiler_params=_cparams(("arbitrary",)),
        name="sample_sink_attn",
    )(qa, kan, van, cak, cav, sink_rows, *masks)


def _sample_dilated_step(qb_ref, kbn_ref, vbn_ref, cbk_ref, cbv_ref, mbc_ref, mbn_ref, b_ref, obk_ref, obv_ref, T):
    width = qb_ref.shape[2]
    n_past = cbk_ref.shape[2]
    n_heads = width // HEAD_DIM
    rows_b = n_heads * T
    row_head = lax.broadcasted_iota(jnp.int32, (rows_b, width), 0) // T
    lane_head = lax.broadcasted_iota(jnp.int32, (rows_b, width), 1) // HEAD_DIM
    diag = row_head == lane_head
    qblk = jnp.where(diag, jnp.concatenate([qb_ref[0]] * n_heads, axis=0), 0.0).astype(BF16)
    s_c = jnp.dot(qblk, cbk_ref[0].astype(BF16), preferred_element_type=F32)
    s_n = lax.dot_general(qblk, _pad_rows(kbn_ref[0], top=True).astype(BF16), NT_DIMS, preferred_element_type=F32)
    p_c, p_n, l = _softmax_parts(s_c, s_n, mbc_ref[...], mbn_ref[...])
    r = lax.dot_general(p_c.astype(BF16), cbv_ref[0].astype(BF16), NT_DIMS, preferred_element_type=F32)
    r = r + jnp.dot(p_n.astype(BF16), _pad_rows(vbn_ref[0], top=True).astype(BF16), preferred_element_type=F32)
    r = jnp.where(diag, r / l, 0.0)
    out = r[0:T]
    for h in range(1, n_heads):
        out = out + r[h * T:(h + 1) * T]
    b_ref[0] = out

    for cache_ref, new_ref, out_ref in ((cbk_ref, kbn_ref, obk_ref), (cbv_ref, vbn_ref, obv_ref)):
        new_t = _pad_rows(new_ref[0], top=False).T
        for h in range(n_heads):
            rows = slice(h * HEAD_DIM, (h + 1) * HEAD_DIM)
            out, last = _rolled(cache_ref[0, rows, :], new_t[rows], T)
            out_ref[0, rows, :n_past - LANES] = out[:, :n_past - LANES]
            out_ref[0, rows, n_past - LANES:] = last


def _layer_norm(z, g, b):
    mu = jnp.mean(z, axis=-1, keepdims=True)
    zc = z - mu
    var = jnp.mean(zc * zc, axis=-1, keepdims=True)
    return zc * lax.rsqrt(var + LN_EPS) * g + b


def _outproj_kernel(x_ref, a_ref, b_ref, wa_ref, wb_ref, g_ref, be_ref, h_ref):
    proj = jnp.dot(a_ref[...].astype(BF16), wa_ref[...], preferred_element_type=F32)
    proj = proj + jnp.dot(b_ref[...].astype(BF16), wb_ref[...], preferred_element_type=F32)
    h_ref[...] = _layer_norm(DN_ALPHA * x_ref[...] + proj, g_ref[...], be_ref[...])


def _outproj_ln(x, a, b, w_a, w_b, g, be, *, tm):
    N, D = x.shape
    assert N % tm == 0
    rows = lambda w: pl.BlockSpec((tm, w), lambda i: (i, 0))
    return pl.pallas_call(
        _outproj_kernel,
        grid=(N // tm,),
        in_specs=[rows(D), rows(a.shape[1]), rows(b.shape[1]), _resident(w_a.shape), _resident(w_b.shape),
                  _resident(g.shape), _resident(be.shape)],
        out_specs=rows(D),
        out_shape=jax.ShapeDtypeStruct((N, D), F32),
        compiler_params=_cparams(("arbitrary",)),
        name="outproj_ln",
    )(x, a, b, w_a, w_b, g, be)


def _merge_outproj_kernel(x_ref, a_ref, o1_ref, l1_ref, o4_ref, l4_ref, o16_ref, l16_ref, e_ref,
                          wa_ref, wb_ref, g_ref, be_ref, h_ref, n4_ref, n16_ref, nl_ref, *, tm):
    m4, m16 = tm // 4, tm // 16
    for r in range(4):
        nl_ref[0, pl.ds(r, m4, stride=4), :] = l4_ref[:, r * LANES:(r + 1) * LANES]
        for g in range(GROUPS_B):
            n4_ref[g, pl.ds(r, m4, stride=4), :] = o4_ref[:, r * QB + g * LANES:r * QB + (g + 1) * LANES].astype(F32)
    for r in range(16):
        nl_ref[1, pl.ds(r, m16, stride=16), :] = l16_ref[:, r * LANES:(r + 1) * LANES]
        for g in range(GROUPS_B):
            n16_ref[g, pl.ds(r, m16, stride=16), :] = (
                o16_ref[:, r * QB + g * LANES:r * QB + (g + 1) * LANES].astype(F32))

    lses = [l1_ref[...], nl_ref[0], nl_ref[1]]
    mx = jnp.maximum(jnp.maximum(lses[0], lses[1]), lses[2])
    ws = [jnp.exp(l - mx) for l in lses]
    inv = 1.0 / (ws[0] + ws[1] + ws[2])
    e = e_ref[...]

    def widen(w):
        hi = w.astype(BF16)
        lo = (w - hi.astype(F32)).astype(BF16)
        return jnp.dot(hi, e, preferred_element_type=F32) + jnp.dot(lo, e, preferred_element_type=F32)

    ws = [widen(w * inv) for w in ws]
    merged = []
    for g in range(GROUPS_B):
        sl = slice(g * LANES, (g + 1) * LANES)
        bg = ws[0][:, sl] * o1_ref[:, sl].astype(F32) + ws[1][:, sl] * n4_ref[g] + ws[2][:, sl] * n16_ref[g]
        merged.append(bg.astype(BF16))
    proj = jnp.dot(a_ref[...], wa_ref[...], preferred_element_type=F32)
    proj = proj + jnp.dot(jnp.concatenate(merged, axis=1), wb_ref[...], preferred_element_type=F32)
    h_ref[...] = _layer_norm(DN_ALPHA * x_ref[...] + proj, g_ref[...], be_ref[...])


def _merge_outproj_ln(x, a, o1, l1, o4, l4, o16, l16, w_a, w_b, g, be, *, tm):
    N, D = x.shape
    assert N % tm == 0 and tm % 256 == 0
    rows = lambda n, w: pl.BlockSpec((n, w), lambda i: (i, 0))
    expander = jnp.asarray(_lse_expander(), BF16)
    return pl.pallas_call(
        functools.partial(_merge_outproj_kernel, tm=tm),
        grid=(N // tm,),
        in_specs=[rows(tm, D), rows(tm, QA), rows(tm, QB), rows(tm, LANES),
                  rows(tm // 4, 4 * QB), rows(tm // 4, 4 * LANES), rows(tm // 16, 16 * QB), rows(tm // 16, 16 * LANES),
                  _resident(expander.shape), _resident(w_a.shape), _resident(w_b.shape),
                  _resident(g.shape), _resident(be.shape)],
        out_specs=rows(tm, D),
        out_shape=jax.ShapeDtypeStruct((N, D), F32),
        scratch_shapes=[pltpu.VMEM((GROUPS_B, tm, LANES), F32), pltpu.VMEM((GROUPS_B, tm, LANES), F32),
                        pltpu.VMEM((2, tm, LANES), F32)],
        compiler_params=_cparams(("arbitrary",)),
        name="merge_outproj_ln",
    )(x, a, o1, l1, o4, l4, o16, l16, expander, w_a, w_b, g, be)


def _ffn_kernel(*refs, tm, n_chunks, n_units, guard_units, T):
    h_hbm, wu_ref, wd_ref, g_ref, be_ref = refs[:5]
    if n_units:
        sample_in = refs[5:12]
        y_ref, b_ref, obk_ref, obv_ref, hb_ref, sem = refs[12:]
    else:
        y_ref, hb_ref, sem = refs[5:]
    i = pl.program_id(0)
    j = pl.program_id(1)

    @pl.when(j == 0)
    def _():
        copy = pltpu.make_async_copy(h_hbm.at[pl.ds(pl.multiple_of(i * tm, tm), tm), :], y_ref, sem.at[0])
        copy.start()
        copy.wait()
        h = y_ref[...]
        hb_ref[...] = h.astype(BF16)
        y_ref[...] = DN_ALPHA * h

    if n_units:
        step_args = sample_in + (b_ref, obk_ref, obv_ref, T)
        if guard_units:
            pl.when(i * n_chunks + j < n_units)(lambda: _sample_dilated_step(*step_args))
        else:
            _sample_dilated_step(*step_args)

    u = jnp.dot(hb_ref[...], wu_ref[...], preferred_element_type=F32)
    u = jnp.maximum(u, 0.0)
    y_ref[...] += jnp.dot((u * u).astype(BF16), wd_ref[...], preferred_element_type=F32)

    @pl.when(j == n_chunks - 1)
    def _():
        y_ref[...] = _layer_norm(y_ref[...], g_ref[...], be_ref[...])


SAMPLE_UNIT_WIDTH = 4 * HEAD_DIM


def _ffn_ln(h, w_up, w_down, g, be, *, tm, tf, sample=None):
    N, D = h.shape
    F = w_up.shape[1]
    assert N % tm == 0 and F % tf == 0
    n_tiles, n_chunks = N // tm, F // tf
    in_specs = [pl.BlockSpec(memory_space=pl.ANY),
                pl.BlockSpec((D, tf), lambda i, j: (0, j)),
                pl.BlockSpec((tf, D), lambda i, j: (j, 0)),
                _resident(g.shape), _resident(be.shape)]
    args = [h, w_up, w_down, g, be]
    out_specs = [pl.BlockSpec((tm, D), lambda i, j: (i, 0))]
    out_shape = [jax.ShapeDtypeStruct((N, D), F32)]
    n_units, T = 0, 0
    if sample is not None:
        qb, kbn, vbn, cbk, cbv = sample
        DB, T, _ = qb.shape
        n_past = cbk.shape[2]
        w = SAMPLE_UNIT_WIDTH
        parts = QB // w
        n_units = DB * parts
        assert n_units <= n_tiles * n_chunks and T % 8 == 0 and T <= LANES and n_past % LANES == 0 and n_past > LANES
        _, _, b_c, b_n = _sample_masks(T, LANES, n_past)
        masks = [jnp.asarray(np.tile(m, (w // HEAD_DIM, 1))) for m in (b_c, b_n)]

        def unit(i, j):
            u = jnp.minimum(i * n_chunks + j, n_units - 1)
            return u // parts, u % parts

        new = pl.BlockSpec((1, T, w), lambda i, j: (unit(i, j)[0], 0, unit(i, j)[1]))
        cache = pl.BlockSpec((1, w, n_past), lambda i, j: (unit(i, j)[0], unit(i, j)[1], 0))
        in_specs += [new, new, new, cache, cache, _resident(masks[0].shape), _resident(masks[1].shape)]
        args += [qb, kbn, vbn, cbk, cbv] + masks
        out_specs += [new, cache, cache]
        out_shape += [jax.ShapeDtypeStruct(qb.shape, F32), jax.ShapeDtypeStruct(cbk.shape, F32),
                      jax.ShapeDtypeStruct(cbv.shape, F32)]
    outs = pl.pallas_call(
        functools.partial(_ffn_kernel, tm=tm, n_chunks=n_chunks, n_units=n_units,
                          guard_units=n_units < n_tiles * n_chunks, T=T),
        grid=(n_tiles, n_chunks),
        in_specs=in_specs,
        out_specs=out_specs,
        out_shape=out_shape,
        scratch_shapes=[pltpu.VMEM((tm, D), BF16), pltpu.SemaphoreType.DMA((1,))],
        compiler_params=_cparams(("arbitrary", "arbitrary")),
        name="ffn_ln_sample" if sample is not None else "ffn_ln",
    )(*args)
    return outs if sample is not None else outs[0]


def _token_tile(n, cap):
    t = min(n, cap)
    while n % t:
        t //= 2
    return t


def _position_minor(cache):
    _, DB, n, H, Dh = cache.shape
    return jnp.transpose(cache[0], (0, 2, 3, 1)).reshape(DB, H * Dh, n)


def _position_major(x, heads):
    DB, _, n = x.shape
    return jnp.transpose(x.reshape(DB, heads, HEAD_DIM, n), (0, 3, 1, 2))[None]


def kernel(x_prompt, x_sample, cache_a_k, cache_a_v, cache_b_k, cache_b_v,
           w_in, sinks, w_out, ln1_g, ln1_b, w_up, w_down, ln2_g, ln2_b):
    B, L, D = x_prompt.shape
    DB, T, _ = x_sample.shape
    assert w_in.shape[0] == 1, "one layer"
    assert DIL_BRANCHES == ((BLK, 1), (4 * BLK, 4), (16 * BLK, 16))
    perm = _qa_head_perm()

    w_in_bf = jnp.concatenate([w_in[0][:, perm], w_in[0][:, QA:]], axis=1).astype(BF16)
    w_out_a = w_out[0][:QA][perm].astype(BF16)
    w_out_b = w_out[0][QA:].astype(BF16)
    w_up_bf = w_up[0].astype(BF16)
    w_down_bf = w_down[0].astype(BF16)
    g1, b1, g2, b2 = ln1_g[0][None], ln1_b[0][None], ln2_g[0][None], ln2_b[0][None]

    cos_p, sin_p = _rope_tables(jnp.arange(L))
    (qa, ka, va, qb1, kb1, vb1, qb4, kb4, vb4, qb16, kb16, vb16, pak, pav, pbk, pbv) = _project(
        x_prompt, w_in_bf, cos_p, sin_p, tm=_token_tile(L, 256), out_dtype=BF16, spread=True)
    a_p = _band_attention(qa, ka, va, stride=1, inclusive=False, sinks=sinks[0], name="sink_attn")
    o1, l1 = _band_attention(qb1, kb1, vb1, stride=1, inclusive=True, emit_lse=True, name="dilated_r1")
    o4, l4 = _band_attention(qb4, kb4, vb4, stride=4, inclusive=True, emit_lse=True, name="dilated_r4")
    o16, l16 = _band_attention(qb16, kb16, vb16, stride=16, inclusive=True, emit_lse=True, name="dilated_r16")
    N_p = B * L
    flat = lambda z: z.reshape(-1, z.shape[-1])
    h_p = _merge_outproj_ln(flat(x_prompt), flat(a_p), flat(o1), flat(l1), flat(o4), flat(l4), flat(o16), flat(l16),
                            w_out_a, w_out_b, g1, b1, tm=_token_tile(L, 256))

    N_s = DB * T
    cos_s, sin_s = _rope_tables(PAST_LEN + jnp.arange(T))
    cos_s = jnp.tile(cos_s, (DB, 1))
    sin_s = jnp.tile(sin_s, (DB, 1))
    sq = _project(x_sample.reshape(1, N_s, D), w_in_bf, cos_s, sin_s, tm=_token_tile(N_s, 256),
                  out_dtype=F32, spread=False)
    qa_s, kan, van, qb_s, kbn, vbn = [z.reshape(DB, T, z.shape[-1]) for z in sq]
    a_s, sak, sav = _sample_sink_attention(qa_s, kan, van, _position_minor(cache_a_k), _position_minor(cache_a_v),
                                           sinks[0])
    y_p, b_s, sbk, sbv = _ffn_ln(h_p, w_up_bf, w_down_bf, g2, b2, tm=_token_tile(N_p, 512), tf=512,
                                 sample=(qb_s, kbn, vbn, _position_minor(cache_b_k), _position_minor(cache_b_v)))
    h_s = _outproj_ln(x_sample.reshape(N_s, D), a_s.reshape(N_s, QA), b_s.reshape(N_s, QB),
                      w_out_a, w_out_b, g1, b1, tm=_token_tile(N_s, 512))
    y_s = _ffn_ln(h_s, w_up_bf, w_down_bf, g2, b2, tm=_token_tile(N_s, 1024), tf=512)

    return (y_p.reshape(B, L, D), y_s.reshape(DB, T, D),
            _position_major(pak, N_KV_A), _position_major(pav, N_KV_A),
            _position_major(pbk, N_HEADS_B), _position_major(pbv, N_HEADS_B),
            _position_major(sak, N_KV_A), _position_major(sav, N_KV_A),
            _position_major(sbk, N_HEADS_B), _position_major(sbv, N_HEADS_B))
```

```python
import functools

import numpy as np
import jax
import jax.numpy as jnp
from jax import lax
from jax.experimental import pallas as pl
from jax.experimental.pallas import tpu as pltpu

F32 = jnp.float32
BF16 = jnp.bfloat16

HEAD_DIM = 64
LANES = 128
N_HEADS_A = 16
N_KV_A = 4
GROUP_A = N_HEADS_A // N_KV_A
N_HEADS_B = 16
WIN_A = 128
DIL_BRANCHES = ((128, 1), (512, 4), (2048, 16))
WIN_B = 2048
BLK = 128
ROPE_THETA = 10000.0
LN_EPS = 1e-5
DN_ALPHA = 2.0 ** 0.25
SCALE = HEAD_DIM ** -0.5
NEG_INF = -1e30
PAST_LEN = 16384
QA = N_HEADS_A * HEAD_DIM
KA = N_KV_A * HEAD_DIM
QB = N_HEADS_B * HEAD_DIM
IN_WIDTH = QA + 2 * KA + 3 * QB
GROUPS_B = QB // LANES
VMEM_LIMIT = 56 * 1024 * 1024
NT_DIMS = (((1,), (1,)), ((), ()))


def _cparams(sem):
    return pltpu.CompilerParams(dimension_semantics=sem, vmem_limit_bytes=VMEM_LIMIT)


def _resident(shape):
    nd = len(shape)
    return pl.BlockSpec(shape, lambda *_: (0,) * nd, pipeline_mode=pl.Buffered(1))


def _qa_head_order():
    return [GROUP_A * (2 * p + half) + j for p in range(N_KV_A // 2) for j in range(GROUP_A) for half in range(2)]


def _reorder_heads(w, axis):
    take = lambda h: lax.slice_in_dim(w, h * HEAD_DIM, (h + 1) * HEAD_DIM, axis=axis)
    return jnp.concatenate([take(h) for h in _qa_head_order()], axis=axis)


def _rope_tables(pos):
    half = HEAD_DIM // 2
    inv = 1.0 / (ROPE_THETA ** (jnp.arange(half, dtype=F32) / half))
    ang = pos.astype(F32)[:, None] * inv[None, :]
    cos = jnp.cos(ang)
    sin = jnp.sin(ang)
    return (jnp.concatenate([cos, cos, cos, cos], -1),
            jnp.concatenate([-sin, sin, -sin, sin], -1))


def _lse_expander():
    e = np.zeros((LANES, QB), np.float32)
    for n in range(GROUPS_B):
        e[n, n * LANES:n * LANES + HEAD_DIM] = 1.0
        e[HEAD_DIM + n, n * LANES + HEAD_DIM:(n + 1) * LANES] = 1.0
    return e


def _proj_kernel(x_ref, w_ref, cos_ref, sin_ref, *refs, tm, spread, cache_rows_a):
    qa_ref, ka_ref, va_ref = refs[:3]
    nat_b = refs[3:6]
    if spread:
        by4, by16, tails, (s1_ref, s2_ref) = refs[6:9], refs[9:12], refs[12:16], refs[16:18]
    xb = x_ref[0].astype(BF16)
    cos = cos_ref[...]
    sin = sin_ref[...]
    lane = lax.broadcasted_iota(jnp.int32, (tm, LANES), 1)
    lo_half = (lane % HEAD_DIM) < (HEAD_DIM // 2)

    def rope(hg):
        swapped = jnp.where(lo_half, pltpu.roll(hg, LANES - HEAD_DIM // 2, 1), pltpu.roll(hg, HEAD_DIM // 2, 1))
        return hg * cos + swapped * sin

    def groups(col0, n):
        h = jnp.dot(xb, w_ref[:, col0:col0 + n * LANES], preferred_element_type=F32)
        return [h[:, g * LANES:(g + 1) * LANES] for g in range(n)]

    def put(ref, g, val):
        ref[0, :, g * LANES:(g + 1) * LANES] = val.astype(ref.dtype)

    def put_spread(which, g, val):
        slab = which * GROUPS_B + g
        n4, n16 = tm // 4, tm // 16
        s1_ref[slab] = val
        for r4 in range(4):
            part = s1_ref[slab, pl.ds(r4, n4, stride=4), :]
            by4[which][0, :, r4 * QB + g * LANES:r4 * QB + (g + 1) * LANES] = part.astype(BF16)
            s2_ref[slab, r4 * n4:(r4 + 1) * n4, :] = part
        for r4 in range(4):
            for a in range(4):
                part = s2_ref[slab, pl.ds(r4 * n4 + a, n16, stride=4), :]
                r16 = 4 * a + r4
                by16[which][0, :, r16 * QB + g * LANES:r16 * QB + (g + 1) * LANES] = part.astype(BF16)

    for c in range(QA // 512):
        for g, hg in enumerate(groups(c * 512, 4)):
            put(qa_ref, c * 4 + g, rope(hg) * SCALE)

    kva = groups(QA, 4)
    ka_g = [rope(kva[0]), rope(kva[1])]
    va_g = [kva[2], kva[3]]
    for g in range(2):
        put(ka_ref, g, ka_g[g])
        put(va_ref, g, va_g[g])
    if spread:
        for g in range(2):
            tails[0][0, g * LANES:(g + 1) * LANES, :] = ka_g[g][tm - cache_rows_a:, :].T
            tails[1][0, g * LANES:(g + 1) * LANES, :] = va_g[g][tm - cache_rows_a:, :].T

    base = QA + 2 * KA
    for which in range(3):
        for c in range(QB // 512):
            vals = groups(base + QB * which + c * 512, 4)
            if which < 2:
                vals = [rope(v) for v in vals]
            if which == 0:
                vals = [v * SCALE for v in vals]
            for g, v in enumerate(vals):
                put(nat_b[which], c * 4 + g, v)
                if spread:
                    put_spread(which, c * 4 + g, v)
            if spread and which > 0:
                for g, v in enumerate(vals):
                    tails[1 + which][0, (c * 4 + g) * LANES:(c * 4 + g + 1) * LANES, :] = v.T


def _project(x, w_in_bf, cos, sin, *, tm, out_dtype, spread):
    B, L, D = x.shape
    nt = L // tm
    assert L % tm == 0
    widths = (QA, KA, KA, QB, QB, QB)
    out_shape = [jax.ShapeDtypeStruct((B, L, w), out_dtype) for w in widths]
    out_specs = [pl.BlockSpec((1, tm, w), lambda b, t: (b, t, 0)) for w in widths]
    scratch = []
    rows_a = min(WIN_A, L)
    rows_b = min(WIN_B, L)
    first_cache_tile = (L - rows_b) // tm
    if spread:
        assert tm % 256 == 0 and rows_a <= tm and (L - rows_b) % tm == 0
        for r in (4, 16):
            out_shape += [jax.ShapeDtypeStruct((B, L // r, r * QB), BF16)] * 3
            out_specs += [pl.BlockSpec((1, tm // r, r * QB), lambda b, t: (b, t, 0))] * 3
        out_shape += [jax.ShapeDtypeStruct((B, KA, rows_a), F32)] * 2
        out_shape += [jax.ShapeDtypeStruct((B, QB, rows_b), F32)] * 2
        out_specs += [pl.BlockSpec((1, KA, rows_a), lambda b, t: (b, 0, 0))] * 2
        out_specs += [pl.BlockSpec((1, QB, tm), lambda b, t: (b, 0, jnp.maximum(t - first_cache_tile, 0)))] * 2
        scratch = [pltpu.VMEM((3 * GROUPS_B, tm, LANES), F32)] * 2
    kern = functools.partial(_proj_kernel, tm=tm, spread=spread, cache_rows_a=rows_a)
    return pl.pallas_call(
        kern,
        grid=(B, nt),
        in_specs=[pl.BlockSpec((1, tm, D), lambda b, t: (b, t, 0)),
                  _resident((D, IN_WIDTH)),
                  pl.BlockSpec((tm, LANES), lambda b, t: (t, 0)),
                  pl.BlockSpec((tm, LANES), lambda b, t: (t, 0))],
        out_specs=out_specs,
        out_shape=out_shape,
        scratch_shapes=scratch,
        compiler_params=_cparams(("arbitrary", "arbitrary")),
        name="proj_spread" if spread else "proj",
    )(x, w_in_bf, cos, sin)


def _band_attn_kernel(*refs, n_kgroups, q_per_k, inclusive, has_sink, emit_lse, n_sub):
    it = iter(refs)
    q_ref, kp_ref, kc_ref, vp_ref, vc_ref = (next(it) for _ in range(5))
    sink_ref = next(it) if has_sink else None
    o_ref = next(it)
    lse_ref = next(it) if emit_lse else None

    mt = pl.program_id(2)
    row = lax.broadcasted_iota(jnp.int32, (2 * BLK, 2 * BLK), 0)
    col = lax.broadcasted_iota(jnp.int32, (2 * BLK, 2 * BLK), 1)
    dist = (row % BLK) + BLK - col
    in_band = (dist >= 0) & ((dist <= BLK) if inclusive else (dist < BLK))
    first_valid = in_band & ((col >= BLK) | (mt > 0))
    lane = lax.broadcasted_iota(jnp.int32, (BLK, LANES), 1)
    lane_lo = lane < HEAD_DIM
    upper_rows = lax.broadcasted_iota(jnp.int32, (2 * BLK, 1), 0) >= BLK
    ones = jnp.ones((2 * BLK, LANES), BF16)
    zero_q = jnp.zeros((BLK, LANES), BF16)
    lse_c = [jnp.zeros((BLK, LANES), F32)] * n_sub

    for n in range(n_kgroups):
        ksl = slice(n * LANES, (n + 1) * LANES)
        kblk = [kp_ref[0, :, ksl]] + [kc_ref[0, i * BLK:(i + 1) * BLK, ksl] for i in range(n_sub)]
        vblk = [vp_ref[0, :, ksl]] + [vc_ref[0, i * BLK:(i + 1) * BLK, ksl] for i in range(n_sub)]
        for sub in range(n_sub):
            rows = slice(sub * BLK, (sub + 1) * BLK)
            valid = first_valid if sub == 0 else in_band
            k2 = jnp.concatenate([kblk[sub], kblk[sub + 1]], axis=0)
            v2e = jnp.concatenate([jnp.concatenate([vblk[sub], vblk[sub + 1]], axis=0), ones], axis=1)
            for j in range(q_per_k):
                g = n * q_per_k + j
                gsl = slice(g * LANES, (g + 1) * LANES)
                qg = q_ref[0, rows, gsl]
                qs = jnp.concatenate([jnp.where(lane_lo, qg, zero_q), jnp.where(lane_lo, zero_q, qg)], axis=0)
                s = lax.dot_general(qs, k2, NT_DIMS, preferred_element_type=F32)
                s = jnp.where(valid, s, NEG_INF)
                m = jnp.max(s, axis=1, keepdims=True)
                if has_sink:
                    lo_head = 2 * GROUP_A * n + j
                    sink = jnp.where(upper_rows, sink_ref[lo_head + GROUP_A], sink_ref[lo_head])
                    m = jnp.maximum(m, sink)
                p = jnp.exp(s - m)
                r = jnp.dot(p.astype(BF16), v2e, preferred_element_type=F32)
                l = r[:, LANES:]
                if has_sink:
                    l = l + jnp.exp(sink - m)
                on = r[:, :LANES] / l
                o_ref[0, rows, gsl] = jnp.where(lane_lo, on[:BLK], on[BLK:]).astype(o_ref.dtype)
                if emit_lse:
                    lse_rows = m + jnp.log(l)
                    lse = jnp.where(lane_lo, lse_rows[:BLK], lse_rows[BLK:])
                    lse_c[sub] = jnp.where((lane == g) | (lane == HEAD_DIM + g), lse, lse_c[sub])
    if emit_lse:
        for sub in range(n_sub):
            lse_ref[0, sub * BLK:(sub + 1) * BLK, :] = lse_c[sub]


def _band_attention(q, k, v, *, stride, inclusive, sinks=None, emit_lse=False, name="band_attn"):
    B, M, _ = q.shape
    Wq, Wk = q.shape[-1] // stride, k.shape[-1] // stride
    assert M % BLK == 0
    n_sub = max(d for d in (1, 2, 4) if (M // BLK) % d == 0)
    tq = n_sub * BLK
    cur = lambda w: pl.BlockSpec((1, tq, w), lambda b, r, m: (b, m, r))
    prv = lambda w: pl.BlockSpec((1, BLK, w), lambda b, r, m: (b, jnp.maximum(m * n_sub - 1, 0), r))
    args = [q, k, k, v, v]
    in_specs = [cur(Wq), prv(Wk), cur(Wk), prv(Wk), cur(Wk)]
    if sinks is not None:
        args.append(sinks)
        in_specs.append(pl.BlockSpec(memory_space=pltpu.SMEM))
    out_shape = [jax.ShapeDtypeStruct(q.shape, BF16)]
    out_specs = [cur(Wq)]
    if emit_lse:
        out_shape.append(jax.ShapeDtypeStruct((B, M, stride * LANES), F32))
        out_specs.append(cur(LANES))
    kern = functools.partial(_band_attn_kernel, n_kgroups=Wk // LANES, q_per_k=Wq // Wk, inclusive=inclusive,
                             has_sink=sinks is not None, emit_lse=emit_lse, n_sub=n_sub)
    outs = pl.pallas_call(
        kern,
        grid=(B, stride, M // tq),
        in_specs=in_specs,
        out_specs=out_specs,
        out_shape=out_shape,
        compiler_params=_cparams(("arbitrary",) * 3),
        name=name,
    )(*args)
    return outs if emit_lse else outs[0]


def _sample_masks(T, n_past_a, n_past_b):
    t = np.arange(T)[:, None]
    dist_c = n_past_a + t - np.arange(n_past_a)[None, :]
    dist_n = t - np.arange(LANES)[None, :]
    a_c = ((dist_c >= 0) & (dist_c < WIN_A)).astype(np.float32)
    a_n = ((dist_n >= 0) & (dist_n < WIN_A)).astype(np.float32)
    dist_c = n_past_b + t - np.arange(n_past_b)[None, :]
    b_c = np.zeros((T, n_past_b), np.float32)
    b_n = np.zeros((T, LANES), np.float32)
    for window, r in DIL_BRANCHES:
        b_c += ((dist_c >= 0) & (dist_c % r == 0) & (dist_c // r <= window // r)).astype(np.float32)
        b_n += ((dist_n >= 0) & (dist_n % r == 0) & (dist_n // r <= window // r)).astype(np.float32)
    return a_c, a_n, b_c, b_n


def _pad_rows(x, top):
    z = jnp.zeros((LANES - x.shape[0], x.shape[1]), F32)
    return jnp.concatenate([x, z] if top else [z, x], axis=0)


def _rolled(cache, new_t, T):
    n = cache.shape[1]
    out = pltpu.roll(cache, n - T, 1)
    keep = lax.broadcasted_iota(jnp.int32, (cache.shape[0], LANES), 1) < LANES - T
    return out, jnp.where(keep, out[:, n - LANES:], new_t)


def _softmax_parts(s_c, s_n, m_c, m_n, sink=None):
    s_c = jnp.where(m_c > 0, s_c, NEG_INF)
    s_n = jnp.where(m_n > 0, s_n, NEG_INF)
    m = jnp.maximum(jnp.max(s_c, axis=1, keepdims=True), jnp.max(s_n, axis=1, keepdims=True))
    if sink is not None:
        m = jnp.maximum(m, sink)
    p_c = m_c * jnp.exp(s_c - m)
    p_n = m_n * jnp.exp(s_n - m)
    l = jnp.sum(p_c, axis=1, keepdims=True) + jnp.sum(p_n, axis=1, keepdims=True)
    if sink is not None:
        l = l + jnp.exp(sink - m)
    return p_c, p_n, l


def _sample_sink_kernel(*refs, T):
    for bi in range(refs[0].shape[0]):
        _sample_sink_one(bi, *refs, T)


def _sample_sink_one(bi, qa_ref, kan_ref, van_ref, cak_ref, cav_ref, sink_ref, mac_ref, man_ref,
                     a_ref, oak_ref, oav_ref, T):
    n_past = cak_ref.shape[2]
    for cache_ref, new_ref, out_ref in ((cak_ref, kan_ref, oak_ref), (cav_ref, van_ref, oav_ref)):
        out, last = _rolled(cache_ref[bi], _pad_rows(new_ref[bi], top=False).T, T)
        if n_past > LANES:
            out_ref[bi, :, :n_past - LANES] = out[:, :n_past - LANES]
        out_ref[bi, :, n_past - LANES:] = last

    lo = lax.broadcasted_iota(jnp.int32, (T, LANES), 1) < HEAD_DIM
    for p in range(N_KV_A // 2):
        ksl = slice(p * LANES, (p + 1) * LANES)
        blocks = []
        for j in range(GROUP_A):
            qg = qa_ref[bi, :, (p * GROUP_A + j) * LANES:(p * GROUP_A + j + 1) * LANES]
            blocks += [jnp.where(lo, qg, 0.0), jnp.where(lo, 0.0, qg)]
        qs = jnp.concatenate(blocks, axis=0).astype(BF16)
        s_c = jnp.dot(qs, cak_ref[bi, ksl, :].astype(BF16), preferred_element_type=F32)
        s_n = lax.dot_general(qs, _pad_rows(kan_ref[bi, :, ksl], top=True).astype(BF16), NT_DIMS,
                              preferred_element_type=F32)
        p_c, p_n, l = _softmax_parts(s_c, s_n, mac_ref[...], man_ref[...], sink_ref[p][:, 0:1])
        r = lax.dot_general(p_c.astype(BF16), cav_ref[bi, ksl, :].astype(BF16), NT_DIMS, preferred_element_type=F32)
        r = r + jnp.dot(p_n.astype(BF16), _pad_rows(van_ref[bi, :, ksl], top=True).astype(BF16),
                        preferred_element_type=F32)
        r = r / l
        for j in range(GROUP_A):
            g = p * GROUP_A + j
            a_ref[bi, :, g * LANES:(g + 1) * LANES] = jnp.where(lo, r[(2 * j) * T:(2 * j + 1) * T],
                                                               r[(2 * j + 1) * T:(2 * j + 2) * T])


def _sample_sink_attention(qa, kan, van, cak, cav, sinks):
    DB, T, _ = qa.shape
    n_past = cak.shape[2]
    assert T % 8 == 0 and T <= LANES and n_past % LANES == 0
    a_c, a_n, _, _ = _sample_masks(T, n_past, LANES)
    masks = [jnp.asarray(np.tile(m, (2 * GROUP_A, 1))) for m in (a_c, a_n)]
    heads = np.asarray([[GROUP_A * (2 * p + half) + j for j in range(GROUP_A) for half in range(2) for _ in range(T)]
                        for p in range(N_KV_A // 2)], np.int32)
    sink_rows = jnp.broadcast_to(sinks[heads][:, :, None], heads.shape + (LANES,)).astype(F32)
    nb = max(d for d in (1, 2, 4) if DB % d == 0)
    per_b = lambda r, w: pl.BlockSpec((nb, r, w), lambda b: (b, 0, 0))
    return pl.pallas_call(
        functools.partial(_sample_sink_kernel, T=T),
        grid=(DB // nb,),
        in_specs=[per_b(T, QA), per_b(T, KA), per_b(T, KA), per_b(KA, n_past), per_b(KA, n_past),
                  _resident(sink_rows.shape), _resident(masks[0].shape), _resident(masks[1].shape)],
        out_specs=[per_b(T, QA), per_b(KA, n_past), per_b(KA, n_past)],
        out_shape=[jax.ShapeDtypeStruct((DB, T, QA), F32), jax.ShapeDtypeStruct(cak.shape, F32),
                   jax.ShapeDtypeStruct(cav.shape, F32)],
        compiler_params=_cparams(("arbitrary",)),
        name="sample_sink_attn",
    )(qa, kan, van, cak, cav, sink_rows, *masks)


def _sample_dilated_step(qb_ref, kbn_ref, vbn_ref, cbk_ref, cbv_ref, mbc_ref, mbn_ref, b_ref, obk_ref, obv_ref, T):
    width = qb_ref.shape[2]
    n_past = cbk_ref.shape[2]
    n_heads = width // HEAD_DIM
    rows_b = n_heads * T
    row_head = lax.broadcasted_iota(jnp.int32, (rows_b, width), 0) // T
    lane_head = lax.broadcasted_iota(jnp.int32, (rows_b, width), 1) // HEAD_DIM
    diag = row_head == lane_head
    qblk = jnp.where(diag, jnp.concatenate([qb_ref[0]] * n_heads, axis=0), 0.0).astype(BF16)
    s_c = jnp.dot(qblk, cbk_ref[0].astype(BF16), preferred_element_type=F32)
    s_n = lax.dot_general(qblk, _pad_rows(kbn_ref[0], top=True).astype(BF16), NT_DIMS, preferred_element_type=F32)
    p_c, p_n, l = _softmax_parts(s_c, s_n, mbc_ref[...], mbn_ref[...])
    r = lax.dot_general(p_c.astype(BF16), cbv_ref[0].astype(BF16), NT_DIMS, preferred_element_type=F32)
    r = r + jnp.dot(p_n.astype(BF16), _pad_rows(vbn_ref[0], top=True).astype(BF16), preferred_element_type=F32)
    r = jnp.where(diag, r / l, 0.0)
    out = r[0:T]
    for h in range(1, n_heads):
        out = out + r[h * T:(h + 1) * T]
    b_ref[0] = out

    for cache_ref, new_ref, out_ref in ((cbk_ref, kbn_ref, obk_ref), (cbv_ref, vbn_ref, obv_ref)):
        new_t = _pad_rows(new_ref[0], top=False).T
        for h in range(n_heads):
            rows = slice(h * HEAD_DIM, (h + 1) * HEAD_DIM)
            out, last = _rolled(cache_ref[0, rows, :], new_t[rows], T)
            out_ref[0, rows, :n_past - LANES] = out[:, :n_past - LANES]
            out_ref[0, rows, n_past - LANES:] = last


def _layer_norm(z, g, b):
    mu = jnp.mean(z, axis=-1, keepdims=True)
    zc = z - mu
    var = jnp.mean(zc * zc, axis=-1, keepdims=True)
    return zc * lax.rsqrt(var + LN_EPS) * g + b


def _outproj_kernel(x_ref, a_ref, b_ref, wa_ref, wb_ref, g_ref, be_ref, h_ref):
    proj = jnp.dot(a_ref[...].astype(BF16), wa_ref[...], preferred_element_type=F32)
    proj = proj + jnp.dot(b_ref[...].astype(BF16), wb_ref[...], preferred_element_type=F32)
    h_ref[...] = _layer_norm(DN_ALPHA * x_ref[...] + proj, g_ref[...], be_ref[...])


def _outproj_ln(x, a, b, w_a, w_b, g, be, *, tm):
    N, D = x.shape
    assert N % tm == 0
    rows = lambda w: pl.BlockSpec((tm, w), lambda i: (i, 0))
    return pl.pallas_call(
        _outproj_kernel,
        grid=(N // tm,),
        in_specs=[rows(D), rows(a.shape[1]), rows(b.shape[1]), _resident(w_a.shape), _resident(w_b.shape),
                  _resident(g.shape), _resident(be.shape)],
        out_specs=rows(D),
        out_shape=jax.ShapeDtypeStruct((N, D), F32),
        compiler_params=_cparams(("arbitrary",)),
        name="outproj_ln",
    )(x, a, b, w_a, w_b, g, be)


def _merge_outproj_kernel(x_ref, a_ref, o1_ref, l1_ref, o4_ref, l4_ref, o16_ref, l16_ref, e_ref,
                          wa_ref, wb_ref, g_ref, be_ref, h_ref, n4_ref, n16_ref, nl_ref, *, tm):
    m4, m16 = tm // 4, tm // 16
    for r in range(4):
        nl_ref[0, pl.ds(r, m4, stride=4), :] = l4_ref[:, r * LANES:(r + 1) * LANES]
        for g in range(GROUPS_B):
            n4_ref[g, pl.ds(r, m4, stride=4), :] = o4_ref[:, r * QB + g * LANES:r * QB + (g + 1) * LANES].astype(F32)
    for r in range(16):
        nl_ref[1, pl.ds(r, m16, stride=16), :] = l16_ref[:, r * LANES:(r + 1) * LANES]
        for g in range(GROUPS_B):
            n16_ref[g, pl.ds(r, m16, stride=16), :] = (
                o16_ref[:, r * QB + g * LANES:r * QB + (g + 1) * LANES].astype(F32))

    lses = [l1_ref[...], nl_ref[0], nl_ref[1]]
    mx = jnp.maximum(jnp.maximum(lses[0], lses[1]), lses[2])
    ws = [jnp.exp(l - mx) for l in lses]
    inv = 1.0 / (ws[0] + ws[1] + ws[2])
    e = e_ref[...]

    def widen(w):
        hi = w.astype(BF16)
        lo = (w - hi.astype(F32)).astype(BF16)
        return jnp.dot(hi, e, preferred_element_type=F32) + jnp.dot(lo, e, preferred_element_type=F32)

    ws = [widen(w * inv) for w in ws]
    merged = []
    for g in range(GROUPS_B):
        sl = slice(g * LANES, (g + 1) * LANES)
        bg = ws[0][:, sl] * o1_ref[:, sl].astype(F32) + ws[1][:, sl] * n4_ref[g] + ws[2][:, sl] * n16_ref[g]
        merged.append(bg.astype(BF16))
    proj = jnp.dot(a_ref[...], wa_ref[...], preferred_element_type=F32)
    proj = proj + jnp.dot(jnp.concatenate(merged, axis=1), wb_ref[...], preferred_element_type=F32)
    h_ref[...] = _layer_norm(DN_ALPHA * x_ref[...] + proj, g_ref[...], be_ref[...])


def _merge_outproj_ln(x, a, o1, l1, o4, l4, o16, l16, w_a, w_b, g, be, *, tm):
    N, D = x.shape
    assert N % tm == 0 and tm % 256 == 0
    rows = lambda n, w: pl.BlockSpec((n, w), lambda i: (i, 0))
    expander = jnp.asarray(_lse_expander(), BF16)
    return pl.pallas_call(
        functools.partial(_merge_outproj_kernel, tm=tm),
        grid=(N // tm,),
        in_specs=[rows(tm, D), rows(tm, QA), rows(tm, QB), rows(tm, LANES),
                  rows(tm // 4, 4 * QB), rows(tm // 4, 4 * LANES), rows(tm // 16, 16 * QB), rows(tm // 16, 16 * LANES),
                  _resident(expander.shape), _resident(w_a.shape), _resident(w_b.shape),
                  _resident(g.shape), _resident(be.shape)],
        out_specs=rows(tm, D),
        out_shape=jax.ShapeDtypeStruct((N, D), F32),
        scratch_shapes=[pltpu.VMEM((GROUPS_B, tm, LANES), F32), pltpu.VMEM((GROUPS_B, tm, LANES), F32),
                        pltpu.VMEM((2, tm, LANES), F32)],
        compiler_params=_cparams(("arbitrary",)),
        name="merge_outproj_ln",
    )(x, a, o1, l1, o4, l4, o16, l16, expander, w_a, w_b, g, be)


def _ffn_kernel(*refs, tm, n_chunks, n_splits, n_units, guard_units, T):
    h_hbm, wu_ref, wd_ref, g_ref, be_ref = refs[:5]
    if n_units:
        sample_in = refs[5:12]
        y_ref, b_ref, obk_ref, obv_ref, hb_ref, sem = refs[-6:]
    else:
        y_ref, hb_ref, sem = refs[5:]
    i = pl.program_id(0)
    j = pl.program_id(1)
    k = pl.program_id(2)

    @pl.when((j == 0) & (k == 0))
    def _():
        copy = pltpu.make_async_copy(h_hbm.at[pl.ds(pl.multiple_of(i * tm, tm), tm), :], y_ref, sem.at[0])
        copy.start()
        copy.wait()
        h = y_ref[...]
        hb_ref[...] = h.astype(BF16)
        y_ref[...] = DN_ALPHA * h

    if n_units:
        step_args = sample_in + (b_ref, obk_ref, obv_ref, T)
        if guard_units:
            pl.when((i * n_chunks + j) * n_splits + k < n_units)(lambda: _sample_dilated_step(*step_args))
        else:
            _sample_dilated_step(*step_args)

    part = tm // n_splits
    rows = pl.ds(pl.multiple_of(k * part, part), part)
    u = jnp.dot(hb_ref[rows, :], wu_ref[...], preferred_element_type=F32)
    u = jnp.maximum(u, 0.0)
    y_ref[rows, :] += jnp.dot((u * u).astype(BF16), wd_ref[...], preferred_element_type=F32)

    @pl.when((j == n_chunks - 1) & (k == n_splits - 1))
    def _():
        y_ref[...] = _layer_norm(y_ref[...], g_ref[...], be_ref[...])


SAMPLE_UNIT_WIDTH = 4 * HEAD_DIM


def _sample_dilated_masks(T, n_past, width):
    _, _, b_c, b_n = _sample_masks(T, LANES, n_past)
    return [jnp.asarray(np.tile(m, (width // HEAD_DIM, 1))) for m in (b_c, b_n)]


def _sample_dilated(qb, kbn, vbn, cbk, cbv, *, b_lo):
    DB, T, _ = qb.shape
    n_past = cbk.shape[2]
    w = QB // 2
    masks = _sample_dilated_masks(T, n_past, w)
    new = pl.BlockSpec((1, T, w), lambda b, p: (b_lo + b, 0, p))
    cache = pl.BlockSpec((1, w, n_past), lambda b, p: (b_lo + b, p, 0))
    return pl.pallas_call(
        functools.partial(_sample_dilated_step, T=T),
        grid=(DB - b_lo, QB // w),
        in_specs=[new, new, new, cache, cache, _resident(masks[0].shape), _resident(masks[1].shape)],
        out_specs=[pl.BlockSpec((1, T, w), lambda b, p: (b, 0, p)), cache, cache],
        out_shape=[jax.ShapeDtypeStruct((DB - b_lo, T, QB), F32), jax.ShapeDtypeStruct(cbk.shape, F32),
                   jax.ShapeDtypeStruct(cbv.shape, F32)],
        compiler_params=_cparams(("arbitrary", "arbitrary")),
        name="sample_dilated",
    )(qb, kbn, vbn, cbk, cbv, *masks)


def _ffn_ln(h, w_up, w_down, g, be, *, tm, tf, n_splits=1, sample=None):
    N, D = h.shape
    F = w_up.shape[1]
    assert N % tm == 0 and F % tf == 0
    assert tm % (8 * n_splits) == 0
    n_tiles, n_chunks = N // tm, F // tf
    n_steps = n_tiles * n_chunks * n_splits
    in_specs = [pl.BlockSpec(memory_space=pl.ANY),
                pl.BlockSpec((D, tf), lambda i, j, k: (0, j)),
                pl.BlockSpec((tf, D), lambda i, j, k: (j, 0)),
                _resident(g.shape), _resident(be.shape)]
    args = [h, w_up, w_down, g, be]
    out_specs = [pl.BlockSpec((tm, D), lambda i, j, k: (i, 0))]
    out_shape = [jax.ShapeDtypeStruct((N, D), F32)]
    n_units, T, aliases = 0, 0, {}
    if sample is not None:
        qb, kbn, vbn, cbk, cbv, n_b, obk, obv = sample
        _, T, _ = qb.shape
        n_past = cbk.shape[2]
        w = SAMPLE_UNIT_WIDTH
        parts = QB // w
        n_units = n_b * parts
        assert 0 < n_units <= n_steps
        masks = _sample_dilated_masks(T, n_past, w)

        def unit(i, j, k):
            u = jnp.minimum((i * n_chunks + j) * n_splits + k, n_units - 1)
            return u // parts, u % parts

        new = pl.BlockSpec((1, T, w), lambda i, j, k: (unit(i, j, k)[0], 0, unit(i, j, k)[1]))
        cache = pl.BlockSpec((1, w, n_past), lambda i, j, k: (unit(i, j, k)[0], unit(i, j, k)[1], 0))
        in_specs += [new, new, new, cache, cache, _resident(masks[0].shape), _resident(masks[1].shape)]
        args += [qb, kbn, vbn, cbk, cbv] + masks
        if obk is not None:
            in_specs += [pl.BlockSpec(memory_space=pl.ANY)] * 2
            args += [obk, obv]
            aliases = {len(args) - 2: 2, len(args) - 1: 3}
        out_specs += [new, cache, cache]
        out_shape += [jax.ShapeDtypeStruct((n_b, T, QB), F32), jax.ShapeDtypeStruct(cbk.shape, F32),
                      jax.ShapeDtypeStruct(cbv.shape, F32)]
    outs = pl.pallas_call(
        functools.partial(_ffn_kernel, tm=tm, n_chunks=n_chunks, n_splits=n_splits, n_units=n_units,
                          guard_units=n_units < n_steps, T=T),
        grid=(n_tiles, n_chunks, n_splits),
        in_specs=in_specs,
        out_specs=out_specs,
        out_shape=out_shape,
        scratch_shapes=[pltpu.VMEM((tm, D), BF16), pltpu.SemaphoreType.DMA((1,))],
        input_output_aliases=aliases,
        compiler_params=_cparams(("arbitrary",) * 3),
        name="ffn_ln_sample" if sample is not None else "ffn_ln",
    )(*args)
    return outs if sample is not None else outs[0]


def _token_tile(n, cap):
    t = min(n, cap)
    while n % t:
        t //= 2
    return t


def _position_minor(cache):
    _, DB, n, H, Dh = cache.shape
    return jnp.transpose(cache[0], (0, 2, 3, 1)).reshape(DB, H * Dh, n)


def _position_major(x, heads):
    DB, _, n = x.shape
    return jnp.transpose(x.reshape(DB, heads, HEAD_DIM, n), (0, 3, 1, 2))[None]


def kernel(x_prompt, x_sample, cache_a_k, cache_a_v, cache_b_k, cache_b_v,
           w_in, sinks, w_out, ln1_g, ln1_b, w_up, w_down, ln2_g, ln2_b):
    B, L, D = x_prompt.shape
    DB, T, _ = x_sample.shape
    assert w_in.shape[0] == 1, "one layer"
    assert DIL_BRANCHES == ((BLK, 1), (4 * BLK, 4), (16 * BLK, 16))

    w_in_bf = jnp.concatenate([_reorder_heads(w_in[0], 1), w_in[0][:, QA:]], axis=1).astype(BF16)
    w_out_a = _reorder_heads(w_out[0], 0).astype(BF16)
    w_out_b = w_out[0][QA:].astype(BF16)
    w_up_bf = w_up[0].astype(BF16)
    w_down_bf = w_down[0].astype(BF16)
    g1, b1, g2, b2 = ln1_g[0][None], ln1_b[0][None], ln2_g[0][None], ln2_b[0][None]

    cos_p, sin_p = _rope_tables(jnp.arange(L))
    (qa, ka, va, qb1, kb1, vb1, qb4, kb4, vb4, qb16, kb16, vb16, pak, pav, pbk, pbv) = _project(
        x_prompt, w_in_bf, cos_p, sin_p, tm=_token_tile(L, 256), out_dtype=BF16, spread=True)
    a_p = _band_attention(qa, ka, va, stride=1, inclusive=False, sinks=sinks[0], name="sink_attn")
    o1, l1 = _band_attention(qb1, kb1, vb1, stride=1, inclusive=True, emit_lse=True, name="dilated_r1")
    o4, l4 = _band_attention(qb4, kb4, vb4, stride=4, inclusive=True, emit_lse=True, name="dilated_r4")
    o16, l16 = _band_attention(qb16, kb16, vb16, stride=16, inclusive=True, emit_lse=True, name="dilated_r16")
    N_p = B * L
    flat = lambda z: z.reshape(-1, z.shape[-1])
    h_p = _merge_outproj_ln(flat(x_prompt), flat(a_p), flat(o1), flat(l1), flat(o4), flat(l4), flat(o16), flat(l16),
                            w_out_a, w_out_b, g1, b1, tm=_token_tile(L, 256))

    N_s = DB * T
    cos_s, sin_s = _rope_tables(PAST_LEN + jnp.arange(T))
    cos_s = jnp.tile(cos_s, (DB, 1))
    sin_s = jnp.tile(sin_s, (DB, 1))
    sq = _project(x_sample.reshape(1, N_s, D), w_in_bf, cos_s, sin_s, tm=_token_tile(N_s, 256),
                  out_dtype=F32, spread=False)
    qa_s, kan, van, qb_s, kbn, vbn = [z.reshape(DB, T, z.shape[-1]) for z in sq]
    a_s, sak, sav = _sample_sink_attention(qa_s, kan, van, _position_minor(cache_a_k), _position_minor(cache_a_v),
                                           sinks[0])
    tm_ffn, tf_ffn, splits = _token_tile(N_p, 1024), 512, 2
    cbk, cbv = _position_minor(cache_b_k), _position_minor(cache_b_v)
    n_b = min(DB, (N_p // tm_ffn) * (w_up.shape[2] // tf_ffn) * splits // (QB // SAMPLE_UNIT_WIDTH))
    b_hi, sbk, sbv = _sample_dilated(qb_s, kbn, vbn, cbk, cbv, b_lo=n_b) if n_b < DB else (None, None, None)
    y_p, b_s, sbk, sbv = _ffn_ln(h_p, w_up_bf, w_down_bf, g2, b2, tm=tm_ffn, tf=tf_ffn, n_splits=splits,
                                 sample=(qb_s, kbn, vbn, cbk, cbv, n_b, sbk, sbv))
    if b_hi is not None:
        b_s = jnp.concatenate([b_s, b_hi], axis=0)
    h_s = _outproj_ln(x_sample.reshape(N_s, D), a_s.reshape(N_s, QA), b_s.reshape(N_s, QB),
                      w_out_a, w_out_b, g1, b1, tm=_token_tile(N_s, 512))
    y_s = _ffn_ln(h_s, w_up_bf, w_down_bf, g2, b2, tm=_token_tile(N_s, 1024), tf=512)

    return (y_p.reshape(B, L, D), y_s.reshape(DB, T, D),
            _position_major(pak, N_KV_A), _position_major(pav, N_KV_A),
            _position_major(pbk, N_HEADS_B), _position_major(pbv, N_HEADS_B),
            _position_major(sak, N_KV_A), _position_major(sav, N_KV_A),
            _position_major(sbk, N_HEADS_B), _position_major(sbv, N_HEADS_B))
```

```python
import functools

import numpy as np
import jax
import jax.numpy as jnp
from jax import lax
from jax.experimental import pallas as pl
from jax.experimental.pallas import tpu as pltpu

F32 = jnp.float32
BF16 = jnp.bfloat16

HEAD_DIM = 64
LANES = 128
N_HEADS_A = 16
N_KV_A = 4
GROUP_A = N_HEADS_A // N_KV_A
N_HEADS_B = 16
WIN_A = 128
DIL_BRANCHES = ((128, 1), (512, 4), (2048, 16))
WIN_B = 2048
BLK = 128
ROPE_THETA = 10000.0
LN_EPS = 1e-5
DN_ALPHA = 2.0 ** 0.25
SCALE = HEAD_DIM ** -0.5
NEG_INF = -1e30
PAST_LEN = 16384
QA = N_HEADS_A * HEAD_DIM
KA = N_KV_A * HEAD_DIM
QB = N_HEADS_B * HEAD_DIM
IN_WIDTH = QA + 2 * KA + 3 * QB
GROUPS_B = QB // LANES
VMEM_LIMIT = 56 * 1024 * 1024
NT_DIMS = (((1,), (1,)), ((), ()))


def _cparams(sem):
    return pltpu.CompilerParams(dimension_semantics=sem, vmem_limit_bytes=VMEM_LIMIT)


def _resident(shape):
    nd = len(shape)
    return pl.BlockSpec(shape, lambda *_: (0,) * nd, pipeline_mode=pl.Buffered(1))


def _qa_head_order():
    return [GROUP_A * (2 * p + half) + j for p in range(N_KV_A // 2) for j in range(GROUP_A) for half in range(2)]


def _reorder_heads(w, axis):
    take = lambda h: lax.slice_in_dim(w, h * HEAD_DIM, (h + 1) * HEAD_DIM, axis=axis)
    return jnp.concatenate([take(h) for h in _qa_head_order()], axis=axis)


def _rope_tables(pos):
    half = HEAD_DIM // 2
    inv = 1.0 / (ROPE_THETA ** (jnp.arange(half, dtype=F32) / half))
    ang = pos.astype(F32)[:, None] * inv[None, :]
    cos = jnp.cos(ang)
    sin = jnp.sin(ang)
    return (jnp.concatenate([cos, cos, cos, cos], -1),
            jnp.concatenate([-sin, sin, -sin, sin], -1))


def _lse_expander():
    e = np.zeros((LANES, QB), np.float32)
    for n in range(GROUPS_B):
        e[n, n * LANES:n * LANES + HEAD_DIM] = 1.0
        e[HEAD_DIM + n, n * LANES + HEAD_DIM:(n + 1) * LANES] = 1.0
    return e


def _proj_kernel(x_ref, w_ref, cos_ref, sin_ref, *refs, tm, spread, cache_rows_a):
    qa_ref, ka_ref, va_ref = refs[:3]
    nat_b = refs[3:6]
    if spread:
        by4, by16, tails, (s1_ref, s2_ref) = refs[6:9], refs[9:12], refs[12:16], refs[16:18]
    xb = x_ref[0].astype(BF16)
    cos = cos_ref[...]
    sin = sin_ref[...]
    lane = lax.broadcasted_iota(jnp.int32, (tm, LANES), 1)
    lo_half = (lane % HEAD_DIM) < (HEAD_DIM // 2)

    def rope(hg):
        swapped = jnp.where(lo_half, pltpu.roll(hg, LANES - HEAD_DIM // 2, 1), pltpu.roll(hg, HEAD_DIM // 2, 1))
        return hg * cos + swapped * sin

    def groups(col0, n):
        h = jnp.dot(xb, w_ref[:, col0:col0 + n * LANES], preferred_element_type=F32)
        return [h[:, g * LANES:(g + 1) * LANES] for g in range(n)]

    def put(ref, g, val):
        ref[0, :, g * LANES:(g + 1) * LANES] = val.astype(ref.dtype)

    def put_spread(which, g, val):
        slab = which * GROUPS_B + g
        n4, n16 = tm // 4, tm // 16
        s1_ref[slab] = val
        for r4 in range(4):
            part = s1_ref[slab, pl.ds(r4, n4, stride=4), :]
            by4[which][0, :, r4 * QB + g * LANES:r4 * QB + (g + 1) * LANES] = part.astype(BF16)
            s2_ref[slab, r4 * n4:(r4 + 1) * n4, :] = part
        for r4 in range(4):
            for a in range(4):
                part = s2_ref[slab, pl.ds(r4 * n4 + a, n16, stride=4), :]
                r16 = 4 * a + r4
                by16[which][0, :, r16 * QB + g * LANES:r16 * QB + (g + 1) * LANES] = part.astype(BF16)

    for c in range(QA // 512):
        for g, hg in enumerate(groups(c * 512, 4)):
            put(qa_ref, c * 4 + g, rope(hg) * SCALE)

    kva = groups(QA, 4)
    ka_g = [rope(kva[0]), rope(kva[1])]
    va_g = [kva[2], kva[3]]
    for g in range(2):
        put(ka_ref, g, ka_g[g])
        put(va_ref, g, va_g[g])
    if spread:
        for g in range(2):
            tails[0][0, g * LANES:(g + 1) * LANES, :] = ka_g[g][tm - cache_rows_a:, :].T
            tails[1][0, g * LANES:(g + 1) * LANES, :] = va_g[g][tm - cache_rows_a:, :].T

    base = QA + 2 * KA
    for which in range(3):
        for c in range(QB // 512):
            vals = groups(base + QB * which + c * 512, 4)
            if which < 2:
                vals = [rope(v) for v in vals]
            if which == 0:
                vals = [v * SCALE for v in vals]
            for g, v in enumerate(vals):
                put(nat_b[which], c * 4 + g, v)
                if spread:
                    put_spread(which, c * 4 + g, v)
            if spread and which > 0:
                for g, v in enumerate(vals):
                    tails[1 + which][0, (c * 4 + g) * LANES:(c * 4 + g + 1) * LANES, :] = v.T


def _project(x, w_in_bf, cos, sin, *, tm, out_dtype, spread):
    B, L, D = x.shape
    nt = L // tm
    assert L % tm == 0
    widths = (QA, KA, KA, QB, QB, QB)
    out_shape = [jax.ShapeDtypeStruct((B, L, w), out_dtype) for w in widths]
    out_specs = [pl.BlockSpec((1, tm, w), lambda b, t: (b, t, 0)) for w in widths]
    scratch = []
    rows_a = min(WIN_A, L)
    rows_b = min(WIN_B, L)
    first_cache_tile = (L - rows_b) // tm
    if spread:
        assert tm % 256 == 0 and rows_a <= tm and (L - rows_b) % tm == 0
        for r in (4, 16):
            out_shape += [jax.ShapeDtypeStruct((B, L // r, r * QB), BF16)] * 3
            out_specs += [pl.BlockSpec((1, tm // r, r * QB), lambda b, t: (b, t, 0))] * 3
        out_shape += [jax.ShapeDtypeStruct((B, KA, rows_a), F32)] * 2
        out_shape += [jax.ShapeDtypeStruct((B, QB, rows_b), F32)] * 2
        out_specs += [pl.BlockSpec((1, KA, rows_a), lambda b, t: (b, 0, 0))] * 2
        out_specs += [pl.BlockSpec((1, QB, tm), lambda b, t: (b, 0, jnp.maximum(t - first_cache_tile, 0)))] * 2
        scratch = [pltpu.VMEM((3 * GROUPS_B, tm, LANES), F32)] * 2
    kern = functools.partial(_proj_kernel, tm=tm, spread=spread, cache_rows_a=rows_a)
    return pl.pallas_call(
        kern,
        grid=(B, nt),
        in_specs=[pl.BlockSpec((1, tm, D), lambda b, t: (b, t, 0)),
                  _resident((D, IN_WIDTH)),
                  pl.BlockSpec((tm, LANES), lambda b, t: (t, 0)),
                  pl.BlockSpec((tm, LANES), lambda b, t: (t, 0))],
        out_specs=out_specs,
        out_shape=out_shape,
        scratch_shapes=scratch,
        compiler_params=_cparams(("arbitrary", "arbitrary")),
        name="proj_spread" if spread else "proj",
    )(x, w_in_bf, cos, sin)


def _band_attn_kernel(*refs, n_kgroups, q_per_k, inclusive, has_sink, emit_lse, n_sub):
    it = iter(refs)
    q_ref, kp_ref, kc_ref, vp_ref, vc_ref = (next(it) for _ in range(5))
    sink_ref = next(it) if has_sink else None
    o_ref = next(it)
    lse_ref = next(it) if emit_lse else None

    mt = pl.program_id(2)
    row = lax.broadcasted_iota(jnp.int32, (2 * BLK, 2 * BLK), 0)
    col = lax.broadcasted_iota(jnp.int32, (2 * BLK, 2 * BLK), 1)
    dist = (row % BLK) + BLK - col
    in_band = (dist >= 0) & ((dist <= BLK) if inclusive else (dist < BLK))
    first_valid = in_band & ((col >= BLK) | (mt > 0))
    lane = lax.broadcasted_iota(jnp.int32, (BLK, LANES), 1)
    lane_lo = lane < HEAD_DIM
    upper_rows = lax.broadcasted_iota(jnp.int32, (2 * BLK, 1), 0) >= BLK
    ones = jnp.ones((2 * BLK, LANES), BF16)
    zero_q = jnp.zeros((BLK, LANES), BF16)
    lse_c = [jnp.zeros((BLK, LANES), F32)] * n_sub

    for n in range(n_kgroups):
        ksl = slice(n * LANES, (n + 1) * LANES)
        kblk = [kp_ref[0, :, ksl]] + [kc_ref[0, i * BLK:(i + 1) * BLK, ksl] for i in range(n_sub)]
        vblk = [vp_ref[0, :, ksl]] + [vc_ref[0, i * BLK:(i + 1) * BLK, ksl] for i in range(n_sub)]
        for sub in range(n_sub):
            rows = slice(sub * BLK, (sub + 1) * BLK)
            valid = first_valid if sub == 0 else in_band
            k2 = jnp.concatenate([kblk[sub], kblk[sub + 1]], axis=0)
            v2e = jnp.concatenate([jnp.concatenate([vblk[sub], vblk[sub + 1]], axis=0), ones], axis=1)
            for j in range(q_per_k):
                g = n * q_per_k + j
                gsl = slice(g * LANES, (g + 1) * LANES)
                qg = q_ref[0, rows, gsl]
                qs = jnp.concatenate([jnp.where(lane_lo, qg, zero_q), jnp.where(lane_lo, zero_q, qg)], axis=0)
                s = lax.dot_general(qs, k2, NT_DIMS, preferred_element_type=F32)
                s = jnp.where(valid, s, NEG_INF)
                m = jnp.max(s, axis=1, keepdims=True)
                if has_sink:
                    lo_head = 2 * GROUP_A * n + j
                    sink = jnp.where(upper_rows, sink_ref[lo_head + GROUP_A], sink_ref[lo_head])
                    m = jnp.maximum(m, sink)
                p = jnp.exp(s - m)
                r = jnp.dot(p.astype(BF16), v2e, preferred_element_type=F32)
                l = r[:, LANES:]
                if has_sink:
                    l = l + jnp.exp(sink - m)
                on = r[:, :LANES] / l
                o_ref[0, rows, gsl] = jnp.where(lane_lo, on[:BLK], on[BLK:]).astype(o_ref.dtype)
                if emit_lse:
                    lse_rows = m + jnp.log(l)
                    lse = jnp.where(lane_lo, lse_rows[:BLK], lse_rows[BLK:])
                    lse_c[sub] = jnp.where((lane == g) | (lane == HEAD_DIM + g), lse, lse_c[sub])
    if emit_lse:
        for sub in range(n_sub):
            lse_ref[0, sub * BLK:(sub + 1) * BLK, :] = lse_c[sub]


def _band_attention(q, k, v, *, stride, inclusive, sinks=None, emit_lse=False, name="band_attn"):
    B, M, _ = q.shape
    Wq, Wk = q.shape[-1] // stride, k.shape[-1] // stride
    assert M % BLK == 0
    n_sub = max(d for d in (1, 2, 4) if (M // BLK) % d == 0)
    tq = n_sub * BLK
    cur = lambda w: pl.BlockSpec((1, tq, w), lambda b, r, m: (b, m, r))
    prv = lambda w: pl.BlockSpec((1, BLK, w), lambda b, r, m: (b, jnp.maximum(m * n_sub - 1, 0), r))
    args = [q, k, k, v, v]
    in_specs = [cur(Wq), prv(Wk), cur(Wk), prv(Wk), cur(Wk)]
    if sinks is not None:
        args.append(sinks)
        in_specs.append(pl.BlockSpec(memory_space=pltpu.SMEM))
    out_shape = [jax.ShapeDtypeStruct(q.shape, BF16)]
    out_specs = [cur(Wq)]
    if emit_lse:
        out_shape.append(jax.ShapeDtypeStruct((B, M, stride * LANES), F32))
        out_specs.append(cur(LANES))
    kern = functools.partial(_band_attn_kernel, n_kgroups=Wk // LANES, q_per_k=Wq // Wk, inclusive=inclusive,
                             has_sink=sinks is not None, emit_lse=emit_lse, n_sub=n_sub)
    outs = pl.pallas_call(
        kern,
        grid=(B, stride, M // tq),
        in_specs=in_specs,
        out_specs=out_specs,
        out_shape=out_shape,
        compiler_params=_cparams(("arbitrary",) * 3),
        name=name,
    )(*args)
    return outs if emit_lse else outs[0]


def _sample_masks(T, n_past_a, n_past_b):
    t = np.arange(T)[:, None]
    dist_c = n_past_a + t - np.arange(n_past_a)[None, :]
    dist_n = t - np.arange(LANES)[None, :]
    a_c = ((dist_c >= 0) & (dist_c < WIN_A)).astype(np.float32)
    a_n = ((dist_n >= 0) & (dist_n < WIN_A)).astype(np.float32)
    dist_c = n_past_b + t - np.arange(n_past_b)[None, :]
    b_c = np.zeros((T, n_past_b), np.float32)
    b_n = np.zeros((T, LANES), np.float32)
    for window, r in DIL_BRANCHES:
        b_c += ((dist_c >= 0) & (dist_c % r == 0) & (dist_c // r <= window // r)).astype(np.float32)
        b_n += ((dist_n >= 0) & (dist_n % r == 0) & (dist_n // r <= window // r)).astype(np.float32)
    return a_c, a_n, b_c, b_n


def _pad_rows(x, top):
    z = jnp.zeros((LANES - x.shape[0], x.shape[1]), F32)
    return jnp.concatenate([x, z] if top else [z, x], axis=0)


def _rolled(cache, new_t, T):
    n = cache.shape[1]
    out = pltpu.roll(cache, n - T, 1)
    keep = lax.broadcasted_iota(jnp.int32, (cache.shape[0], LANES), 1) < LANES - T
    return out, jnp.where(keep, out[:, n - LANES:], new_t)


def _softmax_parts(s_c, s_n, m_c, m_n, sink=None):
    s_c = jnp.where(m_c > 0, s_c, NEG_INF)
    s_n = jnp.where(m_n > 0, s_n, NEG_INF)
    m = jnp.maximum(jnp.max(s_c, axis=1, keepdims=True), jnp.max(s_n, axis=1, keepdims=True))
    if sink is not None:
        m = jnp.maximum(m, sink)
    p_c = m_c * jnp.exp(s_c - m)
    p_n = m_n * jnp.exp(s_n - m)
    l = jnp.sum(p_c, axis=1, keepdims=True) + jnp.sum(p_n, axis=1, keepdims=True)
    if sink is not None:
        l = l + jnp.exp(sink - m)
    return p_c, p_n, l


def _sample_sink_kernel(*refs, T):
    for bi in range(refs[0].shape[0]):
        _sample_sink_one(bi, *refs, T)


def _sample_sink_one(bi, qa_ref, kan_ref, van_ref, cak_ref, cav_ref, sink_ref, mac_ref, man_ref,
                     a_ref, oak_ref, oav_ref, T):
    n_past = cak_ref.shape[2]
    for cache_ref, new_ref, out_ref in ((cak_ref, kan_ref, oak_ref), (cav_ref, van_ref, oav_ref)):
        out, last = _rolled(cache_ref[bi], _pad_rows(new_ref[bi], top=False).T, T)
        if n_past > LANES:
            out_ref[bi, :, :n_past - LANES] = out[:, :n_past - LANES]
        out_ref[bi, :, n_past - LANES:] = last

    lo = lax.broadcasted_iota(jnp.int32, (T, LANES), 1) < HEAD_DIM
    for p in range(N_KV_A // 2):
        ksl = slice(p * LANES, (p + 1) * LANES)
        blocks = []
        for j in range(GROUP_A):
            qg = qa_ref[bi, :, (p * GROUP_A + j) * LANES:(p * GROUP_A + j + 1) * LANES]
            blocks += [jnp.where(lo, qg, 0.0), jnp.where(lo, 0.0, qg)]
        qs = jnp.concatenate(blocks, axis=0).astype(BF16)
        s_c = jnp.dot(qs, cak_ref[bi, ksl, :].astype(BF16), preferred_element_type=F32)
        s_n = lax.dot_general(qs, _pad_rows(kan_ref[bi, :, ksl], top=True).astype(BF16), NT_DIMS,
                              preferred_element_type=F32)
        p_c, p_n, l = _softmax_parts(s_c, s_n, mac_ref[...], man_ref[...], sink_ref[p][:, 0:1])
        r = lax.dot_general(p_c.astype(BF16), cav_ref[bi, ksl, :].astype(BF16), NT_DIMS, preferred_element_type=F32)
        r = r + jnp.dot(p_n.astype(BF16), _pad_rows(van_ref[bi, :, ksl], top=True).astype(BF16),
                        preferred_element_type=F32)
        r = r / l
        for j in range(GROUP_A):
            g = p * GROUP_A + j
            a_ref[bi, :, g * LANES:(g + 1) * LANES] = jnp.where(lo, r[(2 * j) * T:(2 * j + 1) * T],
                                                               r[(2 * j + 1) * T:(2 * j + 2) * T])


def _sample_sink_attention(qa, kan, van, cak, cav, sinks):
    DB, T, _ = qa.shape
    n_past = cak.shape[2]
    assert T % 8 == 0 and T <= LANES and n_past % LANES == 0
    a_c, a_n, _, _ = _sample_masks(T, n_past, LANES)
    masks = [jnp.asarray(np.tile(m, (2 * GROUP_A, 1))) for m in (a_c, a_n)]
    heads = np.asarray([[GROUP_A * (2 * p + half) + j for j in range(GROUP_A) for half in range(2) for _ in range(T)]
                        for p in range(N_KV_A // 2)], np.int32)
    sink_rows = jnp.broadcast_to(sinks[heads][:, :, None], heads.shape + (LANES,)).astype(F32)
    nb = max(d for d in (1, 2, 4) if DB % d == 0)
    per_b = lambda r, w: pl.BlockSpec((nb, r, w), lambda b: (b, 0, 0))
    return pl.pallas_call(
        functools.partial(_sample_sink_kernel, T=T),
        grid=(DB // nb,),
        in_specs=[per_b(T, QA), per_b(T, KA), per_b(T, KA), per_b(KA, n_past), per_b(KA, n_past),
                  _resident(sink_rows.shape), _resident(masks[0].shape), _resident(masks[1].shape)],
        out_specs=[per_b(T, QA), per_b(KA, n_past), per_b(KA, n_past)],
        out_shape=[jax.ShapeDtypeStruct((DB, T, QA), F32), jax.ShapeDtypeStruct(cak.shape, F32),
                   jax.ShapeDtypeStruct(cav.shape, F32)],
        compiler_params=_cparams(("arbitrary",)),
        name="sample_sink_attn",
    )(qa, kan, van, cak, cav, sink_rows, *masks)


def _sample_dilated_step(qb_ref, kbn_ref, vbn_ref, cbk_ref, cbv_ref, mbc_ref, mbn_ref, b_ref, obk_ref, obv_ref, T):
    width = qb_ref.shape[2]
    n_past = cbk_ref.shape[2]
    n_heads = width // HEAD_DIM
    rows_b = n_heads * T
    row_head = lax.broadcasted_iota(jnp.int32, (rows_b, width), 0) // T
    lane_head = lax.broadcasted_iota(jnp.int32, (rows_b, width), 1) // HEAD_DIM
    diag = row_head == lane_head
    qblk = jnp.where(diag, jnp.concatenate([qb_ref[0]] * n_heads, axis=0), 0.0).astype(BF16)
    s_c = jnp.dot(qblk, cbk_ref[0].astype(BF16), preferred_element_type=F32)
    s_n = lax.dot_general(qblk, _pad_rows(kbn_ref[0], top=True).astype(BF16), NT_DIMS, preferred_element_type=F32)
    p_c, p_n, l = _softmax_parts(s_c, s_n, mbc_ref[...], mbn_ref[...])
    r = lax.dot_general(p_c.astype(BF16), cbv_ref[0].astype(BF16), NT_DIMS, preferred_element_type=F32)
    r = r + jnp.dot(p_n.astype(BF16), _pad_rows(vbn_ref[0], top=True).astype(BF16), preferred_element_type=F32)
    r = jnp.where(diag, r / l, 0.0)
    out = r[0:T]
    for h in range(1, n_heads):
        out = out + r[h * T:(h + 1) * T]
    b_ref[0] = out

    for cache_ref, new_ref, out_ref in ((cbk_ref, kbn_ref, obk_ref), (cbv_ref, vbn_ref, obv_ref)):
        new_t = _pad_rows(new_ref[0], top=False).T
        for h in range(n_heads):
            rows = slice(h * HEAD_DIM, (h + 1) * HEAD_DIM)
            out, last = _rolled(cache_ref[0, rows, :], new_t[rows], T)
            out_ref[0, rows, :n_past - LANES] = out[:, :n_past - LANES]
            out_ref[0, rows, n_past - LANES:] = last


def _layer_norm(z, g, b):
    mu = jnp.mean(z, axis=-1, keepdims=True)
    zc = z - mu
    var = jnp.mean(zc * zc, axis=-1, keepdims=True)
    return zc * lax.rsqrt(var + LN_EPS) * g + b


def _outproj_kernel(x_ref, a_ref, b_ref, wa_ref, wb_ref, g_ref, be_ref, h_ref):
    proj = jnp.dot(a_ref[...].astype(BF16), wa_ref[...], preferred_element_type=F32)
    proj = proj + jnp.dot(b_ref[...].astype(BF16), wb_ref[...], preferred_element_type=F32)
    h_ref[...] = _layer_norm(DN_ALPHA * x_ref[...] + proj, g_ref[...], be_ref[...])


def _outproj_ln(x, a, b, w_a, w_b, g, be, *, tm):
    N, D = x.shape
    assert N % tm == 0
    rows = lambda w: pl.BlockSpec((tm, w), lambda i: (i, 0))
    return pl.pallas_call(
        _outproj_kernel,
        grid=(N // tm,),
        in_specs=[rows(D), rows(a.shape[1]), rows(b.shape[1]), _resident(w_a.shape), _resident(w_b.shape),
                  _resident(g.shape), _resident(be.shape)],
        out_specs=rows(D),
        out_shape=jax.ShapeDtypeStruct((N, D), F32),
        compiler_params=_cparams(("arbitrary",)),
        name="outproj_ln",
    )(x, a, b, w_a, w_b, g, be)


def _merge_outproj_kernel(x_ref, a_ref, o1_ref, l1_ref, o4_ref, l4_ref, o16_ref, l16_ref, e_ref,
                          wa_ref, wb_ref, g_ref, be_ref, h_ref, n4_ref, n16_ref, nl_ref, *, tm):
    m4, m16 = tm // 4, tm // 16
    for r in range(4):
        nl_ref[0, pl.ds(r, m4, stride=4), :] = l4_ref[:, r * LANES:(r + 1) * LANES]
        for g in range(GROUPS_B):
            n4_ref[g, pl.ds(r, m4, stride=4), :] = o4_ref[:, r * QB + g * LANES:r * QB + (g + 1) * LANES].astype(F32)
    for r in range(16):
        nl_ref[1, pl.ds(r, m16, stride=16), :] = l16_ref[:, r * LANES:(r + 1) * LANES]
        for g in range(GROUPS_B):
            n16_ref[g, pl.ds(r, m16, stride=16), :] = (
                o16_ref[:, r * QB + g * LANES:r * QB + (g + 1) * LANES].astype(F32))

    lses = [l1_ref[...], nl_ref[0], nl_ref[1]]
    mx = jnp.maximum(jnp.maximum(lses[0], lses[1]), lses[2])
    ws = [jnp.exp(l - mx) for l in lses]
    inv = 1.0 / (ws[0] + ws[1] + ws[2])
    e = e_ref[...]

    def widen(w):
        hi = w.astype(BF16)
        lo = (w - hi.astype(F32)).astype(BF16)
        return jnp.dot(hi, e, preferred_element_type=F32) + jnp.dot(lo, e, preferred_element_type=F32)

    ws = [widen(w * inv) for w in ws]
    merged = []
    for g in range(GROUPS_B):
        sl = slice(g * LANES, (g + 1) * LANES)
        bg = ws[0][:, sl] * o1_ref[:, sl].astype(F32) + ws[1][:, sl] * n4_ref[g] + ws[2][:, sl] * n16_ref[g]
        merged.append(bg.astype(BF16))
    proj = jnp.dot(a_ref[...], wa_ref[...], preferred_element_type=F32)
    proj = proj + jnp.dot(jnp.concatenate(merged, axis=1), wb_ref[...], preferred_element_type=F32)
    h_ref[...] = _layer_norm(DN_ALPHA * x_ref[...] + proj, g_ref[...], be_ref[...])


def _merge_outproj_ln(x, a, o1, l1, o4, l4, o16, l16, w_a, w_b, g, be, *, tm):
    N, D = x.shape
    assert N % tm == 0 and tm % 256 == 0
    rows = lambda n, w: pl.BlockSpec((n, w), lambda i: (i, 0))
    expander = jnp.asarray(_lse_expander(), BF16)
    return pl.pallas_call(
        functools.partial(_merge_outproj_kernel, tm=tm),
        grid=(N // tm,),
        in_specs=[rows(tm, D), rows(tm, QA), rows(tm, QB), rows(tm, LANES),
                  rows(tm // 4, 4 * QB), rows(tm // 4, 4 * LANES), rows(tm // 16, 16 * QB), rows(tm // 16, 16 * LANES),
                  _resident(expander.shape), _resident(w_a.shape), _resident(w_b.shape),
                  _resident(g.shape), _resident(be.shape)],
        out_specs=rows(tm, D),
        out_shape=jax.ShapeDtypeStruct((N, D), F32),
        scratch_shapes=[pltpu.VMEM((GROUPS_B, tm, LANES), F32), pltpu.VMEM((GROUPS_B, tm, LANES), F32),
                        pltpu.VMEM((2, tm, LANES), F32)],
        compiler_params=_cparams(("arbitrary",)),
        name="merge_outproj_ln",
    )(x, a, o1, l1, o4, l4, o16, l16, expander, w_a, w_b, g, be)


def _ffn_kernel(*refs, tm, n_chunks, n_splits, n_units, guard_units, T):
    h_hbm, wu_ref, wd_ref, g_ref, be_ref = refs[:5]
    if n_units:
        sample_in = refs[5:12]
        y_ref, b_ref, obk_ref, obv_ref, hb_ref, sem, kbuf, vbuf, ring_sem = refs[-9:]
    else:
        y_ref, hb_ref, sem = refs[5:]
    i = pl.program_id(0)
    j = pl.program_id(1)
    k = pl.program_id(2)

    @pl.when((j == 0) & (k == 0))
    def _():
        copy = pltpu.make_async_copy(h_hbm.at[pl.ds(pl.multiple_of(i * tm, tm), tm), :], y_ref, sem.at[0])
        copy.start()
        copy.wait()
        h = y_ref[...]
        hb_ref[...] = h.astype(BF16)
        y_ref[...] = DN_ALPHA * h

    if n_units:
        qb_ref, kbn_ref, vbn_ref, cbk_hbm, cbv_hbm, mbc_ref, mbn_ref = sample_in
        step = (i * n_chunks + j) * n_splits + k
        width = kbuf.shape[1]
        parts = cbk_hbm.shape[1] // width

        def fetch(unit, slot):
            rows = pl.ds(pl.multiple_of((unit % parts) * width, width), width)
            return (pltpu.make_async_copy(cbk_hbm.at[unit // parts, rows, :], kbuf.at[slot], ring_sem.at[0, slot]),
                    pltpu.make_async_copy(cbv_hbm.at[unit // parts, rows, :], vbuf.at[slot], ring_sem.at[1, slot]))

        @pl.when(step == 0)
        def _():
            for unit in range(min(RING_SLOTS - 1, n_units)):
                for copy in fetch(unit, unit):
                    copy.start()

        @pl.when(step + RING_SLOTS - 1 < n_units)
        def _():
            for copy in fetch(step + RING_SLOTS - 1, (step + RING_SLOTS - 1) % RING_SLOTS):
                copy.start()

        def sample_step():
            slot = step % RING_SLOTS
            for copy in fetch(step, slot):
                copy.wait()
            _sample_dilated_step(qb_ref, kbn_ref, vbn_ref, kbuf.at[pl.ds(slot, 1)], vbuf.at[pl.ds(slot, 1)],
                                 mbc_ref, mbn_ref, b_ref, obk_ref, obv_ref, T)

        if guard_units:
            pl.when(step < n_units)(sample_step)
        else:
            sample_step()

    part = tm // n_splits
    rows = pl.ds(pl.multiple_of(k * part, part), part)
    u = jnp.dot(hb_ref[rows, :], wu_ref[...], preferred_element_type=F32)
    u = jnp.maximum(u, 0.0)
    y_ref[rows, :] += jnp.dot((u * u).astype(BF16), wd_ref[...], preferred_element_type=F32)

    @pl.when((j == n_chunks - 1) & (k == n_splits - 1))
    def _():
        y_ref[...] = _layer_norm(y_ref[...], g_ref[...], be_ref[...])


SAMPLE_UNIT_WIDTH = 4 * HEAD_DIM
RING_SLOTS = 3


def _sample_dilated_masks(T, n_past, width):
    _, _, b_c, b_n = _sample_masks(T, LANES, n_past)
    return [jnp.asarray(np.tile(m, (width // HEAD_DIM, 1))) for m in (b_c, b_n)]


def _sample_dilated(qb, kbn, vbn, cbk, cbv, *, b_lo):
    DB, T, _ = qb.shape
    n_past = cbk.shape[2]
    w = QB // 2
    masks = _sample_dilated_masks(T, n_past, w)
    new = pl.BlockSpec((1, T, w), lambda b, p: (b_lo + b, 0, p))
    cache = pl.BlockSpec((1, w, n_past), lambda b, p: (b_lo + b, p, 0))
    return pl.pallas_call(
        functools.partial(_sample_dilated_step, T=T),
        grid=(DB - b_lo, QB // w),
        in_specs=[new, new, new, cache, cache, _resident(masks[0].shape), _resident(masks[1].shape)],
        out_specs=[pl.BlockSpec((1, T, w), lambda b, p: (b, 0, p)), cache, cache],
        out_shape=[jax.ShapeDtypeStruct((DB - b_lo, T, QB), F32), jax.ShapeDtypeStruct(cbk.shape, F32),
                   jax.ShapeDtypeStruct(cbv.shape, F32)],
        compiler_params=_cparams(("arbitrary", "arbitrary")),
        name="sample_dilated",
    )(qb, kbn, vbn, cbk, cbv, *masks)


def _ffn_ln(h, w_up, w_down, g, be, *, tm, tf, n_splits=1, sample=None):
    N, D = h.shape
    F = w_up.shape[1]
    assert N % tm == 0 and F % tf == 0
    assert tm % (8 * n_splits) == 0
    n_tiles, n_chunks = N // tm, F // tf
    n_steps = n_tiles * n_chunks * n_splits
    in_specs = [pl.BlockSpec(memory_space=pl.ANY),
                pl.BlockSpec((D, tf), lambda i, j, k: (0, j)),
                pl.BlockSpec((tf, D), lambda i, j, k: (j, 0)),
                _resident(g.shape), _resident(be.shape)]
    args = [h, w_up, w_down, g, be]
    out_specs = [pl.BlockSpec((tm, D), lambda i, j, k: (i, 0))]
    out_shape = [jax.ShapeDtypeStruct((N, D), F32)]
    n_units, T, aliases = 0, 0, {}
    scratch = [pltpu.VMEM((tm, D), BF16), pltpu.SemaphoreType.DMA((1,))]
    if sample is not None:
        qb, kbn, vbn, cbk, cbv, n_b, obk, obv = sample
        _, T, _ = qb.shape
        n_past = cbk.shape[2]
        w = SAMPLE_UNIT_WIDTH
        parts = QB // w
        n_units = n_b * parts
        assert 0 < n_units <= n_steps
        masks = _sample_dilated_masks(T, n_past, w)

        def unit(i, j, k):
            u = jnp.minimum((i * n_chunks + j) * n_splits + k, n_units - 1)
            return u // parts, u % parts

        new = pl.BlockSpec((1, T, w), lambda i, j, k: (unit(i, j, k)[0], 0, unit(i, j, k)[1]))
        cache = pl.BlockSpec((1, w, n_past), lambda i, j, k: (unit(i, j, k)[0], unit(i, j, k)[1], 0))
        hbm = pl.BlockSpec(memory_space=pl.ANY)
        in_specs += [new, new, new, hbm, hbm, _resident(masks[0].shape), _resident(masks[1].shape)]
        args += [qb, kbn, vbn, cbk, cbv] + masks
        scratch += [pltpu.VMEM((RING_SLOTS, w, n_past), F32), pltpu.VMEM((RING_SLOTS, w, n_past), F32),
                    pltpu.SemaphoreType.DMA((2, RING_SLOTS))]
        if obk is not None:
            in_specs += [pl.BlockSpec(memory_space=pl.ANY)] * 2
            args += [obk, obv]
            aliases = {len(args) - 2: 2, len(args) - 1: 3}
        out_specs += [new, cache, cache]
        out_shape += [jax.ShapeDtypeStruct((n_b, T, QB), F32), jax.ShapeDtypeStruct(cbk.shape, F32),
                      jax.ShapeDtypeStruct(cbv.shape, F32)]
    outs = pl.pallas_call(
        functools.partial(_ffn_kernel, tm=tm, n_chunks=n_chunks, n_splits=n_splits, n_units=n_units,
                          guard_units=n_units < n_steps, T=T),
        grid=(n_tiles, n_chunks, n_splits),
        in_specs=in_specs,
        out_specs=out_specs,
        out_shape=out_shape,
        scratch_shapes=scratch,
        input_output_aliases=aliases,
        compiler_params=_cparams(("arbitrary",) * 3),
        name="ffn_ln_sample" if sample is not None else "ffn_ln",
    )(*args)
    return outs if sample is not None else outs[0]


def _token_tile(n, cap):
    t = min(n, cap)
    while n % t:
        t //= 2
    return t


def _position_minor(cache):
    _, DB, n, H, Dh = cache.shape
    return jnp.transpose(cache[0], (0, 2, 3, 1)).reshape(DB, H * Dh, n)


def _position_major(x, heads):
    DB, _, n = x.shape
    return jnp.transpose(x.reshape(DB, heads, HEAD_DIM, n), (0, 3, 1, 2))[None]


def kernel(x_prompt, x_sample, cache_a_k, cache_a_v, cache_b_k, cache_b_v,
           w_in, sinks, w_out, ln1_g, ln1_b, w_up, w_down, ln2_g, ln2_b):
    B, L, D = x_prompt.shape
    DB, T, _ = x_sample.shape
    assert w_in.shape[0] == 1, "one layer"
    assert DIL_BRANCHES == ((BLK, 1), (4 * BLK, 4), (16 * BLK, 16))

    w_in_bf = jnp.concatenate([_reorder_heads(w_in[0], 1), w_in[0][:, QA:]], axis=1).astype(BF16)
    w_out_a = _reorder_heads(w_out[0], 0).astype(BF16)
    w_out_b = w_out[0][QA:].astype(BF16)
    w_up_bf = w_up[0].astype(BF16)
    w_down_bf = w_down[0].astype(BF16)
    g1, b1, g2, b2 = ln1_g[0][None], ln1_b[0][None], ln2_g[0][None], ln2_b[0][None]

    cos_p, sin_p = _rope_tables(jnp.arange(L))
    (qa, ka, va, qb1, kb1, vb1, qb4, kb4, vb4, qb16, kb16, vb16, pak, pav, pbk, pbv) = _project(
        x_prompt, w_in_bf, cos_p, sin_p, tm=_token_tile(L, 256), out_dtype=BF16, spread=True)
    a_p = _band_attention(qa, ka, va, stride=1, inclusive=False, sinks=sinks[0], name="sink_attn")
    o1, l1 = _band_attention(qb1, kb1, vb1, stride=1, inclusive=True, emit_lse=True, name="dilated_r1")
    o4, l4 = _band_attention(qb4, kb4, vb4, stride=4, inclusive=True, emit_lse=True, name="dilated_r4")
    o16, l16 = _band_attention(qb16, kb16, vb16, stride=16, inclusive=True, emit_lse=True, name="dilated_r16")
    N_p = B * L
    flat = lambda z: z.reshape(-1, z.shape[-1])
    h_p = _merge_outproj_ln(flat(x_prompt), flat(a_p), flat(o1), flat(l1), flat(o4), flat(l4), flat(o16), flat(l16),
                            w_out_a, w_out_b, g1, b1, tm=_token_tile(L, 256))

    N_s = DB * T
    cos_s, sin_s = _rope_tables(PAST_LEN + jnp.arange(T))
    cos_s = jnp.tile(cos_s, (DB, 1))
    sin_s = jnp.tile(sin_s, (DB, 1))
    sq = _project(x_sample.reshape(1, N_s, D), w_in_bf, cos_s, sin_s, tm=_token_tile(N_s, 256),
                  out_dtype=F32, spread=False)
    qa_s, kan, van, qb_s, kbn, vbn = [z.reshape(DB, T, z.shape[-1]) for z in sq]
    a_s, sak, sav = _sample_sink_attention(qa_s, kan, van, _position_minor(cache_a_k), _position_minor(cache_a_v),
                                           sinks[0])
    tm_ffn, tf_ffn, splits = _token_tile(N_p, 1024), 512, 2
    cbk, cbv = _position_minor(cache_b_k), _position_minor(cache_b_v)
    n_b = min(DB, (N_p // tm_ffn) * (w_up.shape[2] // tf_ffn) * splits // (QB // SAMPLE_UNIT_WIDTH))
    b_hi, sbk, sbv = _sample_dilated(qb_s, kbn, vbn, cbk, cbv, b_lo=n_b) if n_b < DB else (None, None, None)
    y_p, b_s, sbk, sbv = _ffn_ln(h_p, w_up_bf, w_down_bf, g2, b2, tm=tm_ffn, tf=tf_ffn, n_splits=splits,
                                 sample=(qb_s, kbn, vbn, cbk, cbv, n_b, sbk, sbv))
    if b_hi is not None:
        b_s = jnp.concatenate([b_s, b_hi], axis=0)
    h_s = _outproj_ln(x_sample.reshape(N_s, D), a_s.reshape(N_s, QA), b_s.reshape(N_s, QB),
                      w_out_a, w_out_b, g1, b1, tm=_token_tile(N_s, 512))
    y_s = _ffn_ln(h_s, w_up_bf, w_down_bf, g2, b2, tm=_token_tile(N_s, 1024), tf=512)

    return (y_p.reshape(B, L, D), y_s.reshape(DB, T, D),
            _position_major(pak, N_KV_A), _position_major(pav, N_KV_A),
            _position_major(pbk, N_HEADS_B), _position_major(pbv, N_HEADS_B),
            _position_major(sak, N_KV_A), _position_major(sav, N_KV_A),
            _position_major(sbk, N_HEADS_B), _position_major(sbv, N_HEADS_B))
```

```python
import functools

import numpy as np
import jax
import jax.numpy as jnp
from jax import lax
from jax.experimental import pallas as pl
from jax.experimental.pallas import tpu as pltpu

F32 = jnp.float32
BF16 = jnp.bfloat16

HEAD_DIM = 64
LANES = 128
N_HEADS_A = 16
N_KV_A = 4
GROUP_A = N_HEADS_A // N_KV_A
N_HEADS_B = 16
WIN_A = 128
DIL_BRANCHES = ((128, 1), (512, 4), (2048, 16))
WIN_B = 2048
BLK = 128
ROPE_THETA = 10000.0
LN_EPS = 1e-5
DN_ALPHA = 2.0 ** 0.25
SCALE = HEAD_DIM ** -0.5
LOG2E = 1.4426950408889634
NEG_INF = -1e30
PAST_LEN = 16384
QA = N_HEADS_A * HEAD_DIM
KA = N_KV_A * HEAD_DIM
QB = N_HEADS_B * HEAD_DIM
IN_WIDTH = QA + 2 * KA + 3 * QB
GROUPS_B = QB // LANES
VMEM_LIMIT = 56 * 1024 * 1024
NT_DIMS = (((1,), (1,)), ((), ()))


def _cparams(sem):
    return pltpu.CompilerParams(dimension_semantics=sem, vmem_limit_bytes=VMEM_LIMIT)


def _resident(shape):
    nd = len(shape)
    return pl.BlockSpec(shape, lambda *_: (0,) * nd, pipeline_mode=pl.Buffered(1))


def _qa_head_order():
    return [GROUP_A * (2 * p + half) + j for p in range(N_KV_A // 2) for j in range(GROUP_A) for half in range(2)]


def _reorder_heads(w, axis):
    take = lambda h: lax.slice_in_dim(w, h * HEAD_DIM, (h + 1) * HEAD_DIM, axis=axis)
    return jnp.concatenate([take(h) for h in _qa_head_order()], axis=axis)


def _rope_tables(pos):
    half = HEAD_DIM // 2
    inv = 1.0 / (ROPE_THETA ** (jnp.arange(half, dtype=F32) / half))
    ang = pos.astype(F32)[:, None] * inv[None, :]
    cos = jnp.cos(ang)
    sin = jnp.sin(ang)
    return (jnp.concatenate([cos, cos, cos, cos], -1),
            jnp.concatenate([-sin, sin, -sin, sin], -1))


def _lse_expander():
    e = np.zeros((LANES, QB), np.float32)
    for n in range(GROUPS_B):
        e[n, n * LANES:n * LANES + HEAD_DIM] = 1.0
        e[HEAD_DIM + n, n * LANES + HEAD_DIM:(n + 1) * LANES] = 1.0
    return e


def _proj_kernel(x_ref, w_ref, wqa_ref, cos_ref, sin_ref, *refs, tm, spread, cache_rows_a, q_scale):
    qa_ref, ka_ref, va_ref = refs[:3]
    nat_b = refs[3:6]
    if spread:
        by4, by16, tails, (s1_ref, s2_ref) = refs[6:9], refs[9:12], refs[12:16], refs[16:18]
    xb = x_ref[0].astype(BF16)
    cos = cos_ref[...]
    sin = sin_ref[...]
    lane = lax.broadcasted_iota(jnp.int32, (tm, LANES), 1)
    lo_half = (lane % HEAD_DIM) < (HEAD_DIM // 2)

    def rope(hg):
        swapped = jnp.where(lo_half, pltpu.roll(hg, LANES - HEAD_DIM // 2, 1), pltpu.roll(hg, HEAD_DIM // 2, 1))
        return hg * cos + swapped * sin

    def groups(col0, n, ref=w_ref):
        h = jnp.dot(xb, ref[:, col0:col0 + n * LANES], preferred_element_type=F32)
        return [h[:, g * LANES:(g + 1) * LANES] for g in range(n)]

    def put(ref, g, val):
        ref[0, :, g * LANES:(g + 1) * LANES] = val.astype(ref.dtype)

    def put_spread(which, g, val):
        slab = which * GROUPS_B + g
        n4, n16 = tm // 4, tm // 16
        s1_ref[slab] = val
        for r4 in range(4):
            part = s1_ref[slab, pl.ds(r4, n4, stride=4), :]
            by4[which][0, :, r4 * QB + g * LANES:r4 * QB + (g + 1) * LANES] = part.astype(BF16)
            s2_ref[slab, r4 * n4:(r4 + 1) * n4, :] = part
        for r4 in range(4):
            for a in range(4):
                part = s2_ref[slab, pl.ds(r4 * n4 + a, n16, stride=4), :]
                r16 = 4 * a + r4
                by16[which][0, :, r16 * QB + g * LANES:r16 * QB + (g + 1) * LANES] = part.astype(BF16)

    for c in range(QA // 512):
        for g, hg in enumerate(groups(c * 512, 4, wqa_ref)):
            put(qa_ref, c * 4 + g, rope(hg) * q_scale)

    kva = groups(QA, 4)
    ka_g = [rope(kva[0]), rope(kva[1])]
    va_g = [kva[2], kva[3]]
    for g in range(2):
        put(ka_ref, g, ka_g[g])
        put(va_ref, g, va_g[g])
    if spread:
        for g in range(2):
            tails[0][0, g * LANES:(g + 1) * LANES, :] = ka_g[g][tm - cache_rows_a:, :].T
            tails[1][0, g * LANES:(g + 1) * LANES, :] = va_g[g][tm - cache_rows_a:, :].T

    base = QA + 2 * KA
    for which in range(3):
        for c in range(QB // 512):
            vals = groups(base + QB * which + c * 512, 4)
            if which < 2:
                vals = [rope(v) for v in vals]
            if which == 0:
                vals = [v * q_scale for v in vals]
            for g, v in enumerate(vals):
                put(nat_b[which], c * 4 + g, v)
                if spread:
                    put_spread(which, c * 4 + g, v)
            if spread and which > 0:
                for g, v in enumerate(vals):
                    tails[1 + which][0, (c * 4 + g) * LANES:(c * 4 + g + 1) * LANES, :] = v.T


def _project(x, w_in_bf, w_qa_bf, cos, sin, *, tm, out_dtype, spread, q_scale):
    B, L, D = x.shape
    nt = L // tm
    assert L % tm == 0
    widths = (QA, KA, KA, QB, QB, QB)
    out_shape = [jax.ShapeDtypeStruct((B, L, w), out_dtype) for w in widths]
    out_specs = [pl.BlockSpec((1, tm, w), lambda b, t: (b, t, 0)) for w in widths]
    scratch = []
    rows_a = min(WIN_A, L)
    rows_b = min(WIN_B, L)
    first_cache_tile = (L - rows_b) // tm
    if spread:
        assert tm % 256 == 0 and rows_a <= tm and (L - rows_b) % tm == 0
        for r in (4, 16):
            out_shape += [jax.ShapeDtypeStruct((B, L // r, r * QB), BF16)] * 3
            out_specs += [pl.BlockSpec((1, tm // r, r * QB), lambda b, t: (b, t, 0))] * 3
        out_shape += [jax.ShapeDtypeStruct((B, KA, rows_a), F32)] * 2
        out_shape += [jax.ShapeDtypeStruct((B, QB, rows_b), F32)] * 2
        out_specs += [pl.BlockSpec((1, KA, rows_a), lambda b, t: (b, 0, 0))] * 2
        out_specs += [pl.BlockSpec((1, QB, tm), lambda b, t: (b, 0, jnp.maximum(t - first_cache_tile, 0)))] * 2
        scratch = [pltpu.VMEM((3 * GROUPS_B, tm, LANES), F32)] * 2
    kern = functools.partial(_proj_kernel, tm=tm, spread=spread, cache_rows_a=rows_a, q_scale=q_scale)
    return pl.pallas_call(
        kern,
        grid=(B, nt),
        in_specs=[pl.BlockSpec((1, tm, D), lambda b, t: (b, t, 0)),
                  _resident((D, IN_WIDTH)), _resident((D, QA)),
                  pl.BlockSpec((tm, LANES), lambda b, t: (t, 0)),
                  pl.BlockSpec((tm, LANES), lambda b, t: (t, 0))],
        out_specs=out_specs,
        out_shape=out_shape,
        scratch_shapes=scratch,
        compiler_params=_cparams(("arbitrary", "arbitrary")),
        name="proj_spread" if spread else "proj",
    )(x, w_in_bf, w_qa_bf, cos, sin)


def _band_attn_kernel(*refs, n_kgroups, q_per_k, inclusive, has_sink, emit_lse, n_sub):
    it = iter(refs)
    q_ref, kp_ref, kc_ref, vp_ref, vc_ref = (next(it) for _ in range(5))
    sink_ref = next(it) if has_sink else None
    o_ref = next(it)
    lse_ref = next(it) if emit_lse else None

    mt = pl.program_id(2)
    row = lax.broadcasted_iota(jnp.int32, (2 * BLK, 2 * BLK), 0)
    col = lax.broadcasted_iota(jnp.int32, (2 * BLK, 2 * BLK), 1)
    dist = (row % BLK) + BLK - col
    in_band = (dist >= 0) & ((dist <= BLK) if inclusive else (dist < BLK))
    first_valid = in_band & ((col >= BLK) | (mt > 0))
    lane = lax.broadcasted_iota(jnp.int32, (BLK, LANES), 1)
    lane_lo = lane < HEAD_DIM
    upper_rows = lax.broadcasted_iota(jnp.int32, (2 * BLK, 1), 0) >= BLK
    ones = jnp.ones((2 * BLK, LANES), BF16)
    zero_q = jnp.zeros((BLK, LANES), BF16)
    lse_c = [jnp.zeros((BLK, LANES), F32)] * n_sub

    for n in range(n_kgroups):
        ksl = slice(n * LANES, (n + 1) * LANES)
        kblk = [kp_ref[0, :, ksl]] + [kc_ref[0, i * BLK:(i + 1) * BLK, ksl] for i in range(n_sub)]
        vblk = [vp_ref[0, :, ksl]] + [vc_ref[0, i * BLK:(i + 1) * BLK, ksl] for i in range(n_sub)]
        for sub in range(n_sub):
            rows = slice(sub * BLK, (sub + 1) * BLK)
            valid = first_valid if sub == 0 else in_band
            k2 = jnp.concatenate([kblk[sub], kblk[sub + 1]], axis=0)
            v2e = jnp.concatenate([jnp.concatenate([vblk[sub], vblk[sub + 1]], axis=0), ones], axis=1)
            for j in range(q_per_k):
                g = n * q_per_k + j
                gsl = slice(g * LANES, (g + 1) * LANES)
                qg = q_ref[0, rows, gsl]
                qs = jnp.concatenate([jnp.where(lane_lo, qg, zero_q), jnp.where(lane_lo, zero_q, qg)], axis=0)
                s = lax.dot_general(qs, k2, NT_DIMS, preferred_element_type=F32)
                s = jnp.where(valid, s, NEG_INF)
                m = jnp.max(s, axis=1, keepdims=True)
                if has_sink:
                    lo_head = 2 * GROUP_A * n + j
                    sink = jnp.where(upper_rows, sink_ref[lo_head + GROUP_A], sink_ref[lo_head]) * LOG2E
                    m = jnp.maximum(m, sink)
                p = jnp.exp2(s - m)
                r = jnp.dot(p.astype(BF16), v2e, preferred_element_type=F32)
                l = r[:, LANES:]
                if has_sink:
                    l = l + jnp.exp2(sink - m)
                on = r[:, :LANES] / l
                o_ref[0, rows, gsl] = jnp.where(lane_lo, on[:BLK], on[BLK:]).astype(o_ref.dtype)
                if emit_lse:
                    lse_rows = m + jnp.log2(l)
                    lse = jnp.where(lane_lo, lse_rows[:BLK], lse_rows[BLK:])
                    lse_c[sub] = jnp.where((lane == g) | (lane == HEAD_DIM + g), lse, lse_c[sub])
    if emit_lse:
        for sub in range(n_sub):
            lse_ref[0, sub * BLK:(sub + 1) * BLK, :] = lse_c[sub]


def _band_attention(q, k, v, *, stride, inclusive, sinks=None, emit_lse=False, name="band_attn"):
    B, M, _ = q.shape
    Wq, Wk = q.shape[-1] // stride, k.shape[-1] // stride
    assert M % BLK == 0
    n_sub = max(d for d in (1, 2, 4) if (M // BLK) % d == 0)
    tq = n_sub * BLK
    cur = lambda w: pl.BlockSpec((1, tq, w), lambda b, r, m: (b, m, r))
    prv = lambda w: pl.BlockSpec((1, BLK, w), lambda b, r, m: (b, jnp.maximum(m * n_sub - 1, 0), r))
    args = [q, k, k, v, v]
    in_specs = [cur(Wq), prv(Wk), cur(Wk), prv(Wk), cur(Wk)]
    if sinks is not None:
        args.append(sinks)
        in_specs.append(pl.BlockSpec(memory_space=pltpu.SMEM))
    out_shape = [jax.ShapeDtypeStruct(q.shape, BF16)]
    out_specs = [cur(Wq)]
    if emit_lse:
        out_shape.append(jax.ShapeDtypeStruct((B, M, stride * LANES), F32))
        out_specs.append(cur(LANES))
    kern = functools.partial(_band_attn_kernel, n_kgroups=Wk // LANES, q_per_k=Wq // Wk, inclusive=inclusive,
                             has_sink=sinks is not None, emit_lse=emit_lse, n_sub=n_sub)
    outs = pl.pallas_call(
        kern,
        grid=(B, stride, M // tq),
        in_specs=in_specs,
        out_specs=out_specs,
        out_shape=out_shape,
        compiler_params=_cparams(("arbitrary",) * 3),
        name=name,
    )(*args)
    return outs if emit_lse else outs[0]


def _sample_masks(T, n_past_a, n_past_b):
    t = np.arange(T)[:, None]
    dist_c = n_past_a + t - np.arange(n_past_a)[None, :]
    dist_n = t - np.arange(LANES)[None, :]
    a_c = ((dist_c >= 0) & (dist_c < WIN_A)).astype(np.float32)
    a_n = ((dist_n >= 0) & (dist_n < WIN_A)).astype(np.float32)
    dist_c = n_past_b + t - np.arange(n_past_b)[None, :]
    b_c = np.zeros((T, n_past_b), np.float32)
    b_n = np.zeros((T, LANES), np.float32)
    for window, r in DIL_BRANCHES:
        b_c += ((dist_c >= 0) & (dist_c % r == 0) & (dist_c // r <= window // r)).astype(np.float32)
        b_n += ((dist_n >= 0) & (dist_n % r == 0) & (dist_n // r <= window // r)).astype(np.float32)
    return a_c, a_n, b_c, b_n


def _pad_rows(x, top):
    z = jnp.zeros((LANES - x.shape[0], x.shape[1]), F32)
    return jnp.concatenate([x, z] if top else [z, x], axis=0)


def _rolled(cache, new_t, T):
    n = cache.shape[1]
    out = pltpu.roll(cache, n - T, 1)
    keep = lax.broadcasted_iota(jnp.int32, (cache.shape[0], LANES), 1) < LANES - T
    return out, jnp.where(keep, out[:, n - LANES:], new_t)


def _softmax_parts(s_c, s_n, m_c, m_n, sink=None):
    s_c = jnp.where(m_c > 0, s_c, NEG_INF)
    s_n = jnp.where(m_n > 0, s_n, NEG_INF)
    m = jnp.maximum(jnp.max(s_c, axis=1, keepdims=True), jnp.max(s_n, axis=1, keepdims=True))
    if sink is not None:
        m = jnp.maximum(m, sink)
    p_c = m_c * jnp.exp(s_c - m)
    p_n = m_n * jnp.exp(s_n - m)
    l = jnp.sum(p_c, axis=1, keepdims=True) + jnp.sum(p_n, axis=1, keepdims=True)
    if sink is not None:
        l = l + jnp.exp(sink - m)
    return p_c, p_n, l


def _sample_sink_kernel(*refs, T):
    for bi in range(refs[0].shape[0]):
        _sample_sink_one(bi, *refs, T)


def _sample_sink_one(bi, qa_ref, kan_ref, van_ref, cak_ref, cav_ref, sink_ref, mac_ref, man_ref,
                     a_ref, oak_ref, oav_ref, T):
    n_past = cak_ref.shape[2]
    for cache_ref, new_ref, out_ref in ((cak_ref, kan_ref, oak_ref), (cav_ref, van_ref, oav_ref)):
        out, last = _rolled(cache_ref[bi], _pad_rows(new_ref[bi], top=False).T, T)
        if n_past > LANES:
            out_ref[bi, :, :n_past - LANES] = out[:, :n_past - LANES]
        out_ref[bi, :, n_past - LANES:] = last

    lo = lax.broadcasted_iota(jnp.int32, (T, LANES), 1) < HEAD_DIM
    for p in range(N_KV_A // 2):
        ksl = slice(p * LANES, (p + 1) * LANES)
        blocks = []
        for j in range(GROUP_A):
            qg = qa_ref[bi, :, (p * GROUP_A + j) * LANES:(p * GROUP_A + j + 1) * LANES]
            blocks += [jnp.where(lo, qg, 0.0), jnp.where(lo, 0.0, qg)]
        qs = jnp.concatenate(blocks, axis=0).astype(BF16)
        s_c = jnp.dot(qs, cak_ref[bi, ksl, :].astype(BF16), preferred_element_type=F32)
        s_n = lax.dot_general(qs, _pad_rows(kan_ref[bi, :, ksl], top=True).astype(BF16), NT_DIMS,
                              preferred_element_type=F32)
        p_c, p_n, l = _softmax_parts(s_c, s_n, mac_ref[...], man_ref[...], sink_ref[p][:, 0:1])
        r = lax.dot_general(p_c.astype(BF16), cav_ref[bi, ksl, :].astype(BF16), NT_DIMS, preferred_element_type=F32)
        r = r + jnp.dot(p_n.astype(BF16), _pad_rows(van_ref[bi, :, ksl], top=True).astype(BF16),
                        preferred_element_type=F32)
        r = r / l
        for j in range(GROUP_A):
            g = p * GROUP_A + j
            a_ref[bi, :, g * LANES:(g + 1) * LANES] = jnp.where(lo, r[(2 * j) * T:(2 * j + 1) * T],
                                                               r[(2 * j + 1) * T:(2 * j + 2) * T])


def _sample_sink_attention(qa, kan, van, cak, cav, sinks):
    DB, T, _ = qa.shape
    n_past = cak.shape[2]
    assert T % 8 == 0 and T <= LANES and n_past % LANES == 0
    a_c, a_n, _, _ = _sample_masks(T, n_past, LANES)
    masks = [jnp.asarray(np.tile(m, (2 * GROUP_A, 1))) for m in (a_c, a_n)]
    heads = np.asarray([[GROUP_A * (2 * p + half) + j for j in range(GROUP_A) for half in range(2) for _ in range(T)]
                        for p in range(N_KV_A // 2)], np.int32)
    sink_rows = jnp.broadcast_to(sinks[heads][:, :, None], heads.shape + (LANES,)).astype(F32)
    nb = max(d for d in (1, 2, 4, 8) if DB % d == 0)
    per_b = lambda r, w: pl.BlockSpec((nb, r, w), lambda b: (b, 0, 0))
    return pl.pallas_call(
        functools.partial(_sample_sink_kernel, T=T),
        grid=(DB // nb,),
        in_specs=[per_b(T, QA), per_b(T, KA), per_b(T, KA), per_b(KA, n_past), per_b(KA, n_past),
                  _resident(sink_rows.shape), _resident(masks[0].shape), _resident(masks[1].shape)],
        out_specs=[per_b(T, QA), per_b(KA, n_past), per_b(KA, n_past)],
        out_shape=[jax.ShapeDtypeStruct((DB, T, QA), F32), jax.ShapeDtypeStruct(cak.shape, F32),
                   jax.ShapeDtypeStruct(cav.shape, F32)],
        compiler_params=_cparams(("arbitrary",)),
        name="sample_sink_attn",
    )(qa, kan, van, cak, cav, sink_rows, *masks)


def _sample_dilated_step(qb_ref, kbn_ref, vbn_ref, cbk_ref, cbv_ref, mbc_ref, mbn_ref, b_ref, obk_ref, obv_ref, T):
    width = qb_ref.shape[2]
    n_past = cbk_ref.shape[2]
    n_heads = width // HEAD_DIM
    rows_b = n_heads * T
    row_head = lax.broadcasted_iota(jnp.int32, (rows_b, width), 0) // T
    lane_head = lax.broadcasted_iota(jnp.int32, (rows_b, width), 1) // HEAD_DIM
    diag = row_head == lane_head
    qblk = jnp.where(diag, jnp.concatenate([qb_ref[0]] * n_heads, axis=0), 0.0).astype(BF16)
    s_c = jnp.dot(qblk, cbk_ref[0].astype(BF16), preferred_element_type=F32)
    s_n = lax.dot_general(qblk, _pad_rows(kbn_ref[0], top=True).astype(BF16), NT_DIMS, preferred_element_type=F32)
    p_c, p_n, l = _softmax_parts(s_c, s_n, mbc_ref[...], mbn_ref[...])
    r = lax.dot_general(p_c.astype(BF16), cbv_ref[0].astype(BF16), NT_DIMS, preferred_element_type=F32)
    r = r + jnp.dot(p_n.astype(BF16), _pad_rows(vbn_ref[0], top=True).astype(BF16), preferred_element_type=F32)
    r = jnp.where(diag, r / l, 0.0)
    out = r[0:T]
    for h in range(1, n_heads):
        out = out + r[h * T:(h + 1) * T]
    b_ref[0] = out

    for cache_ref, new_ref, out_ref in ((cbk_ref, kbn_ref, obk_ref), (cbv_ref, vbn_ref, obv_ref)):
        new_t = _pad_rows(new_ref[0], top=False).T
        for h in range(n_heads):
            rows = slice(h * HEAD_DIM, (h + 1) * HEAD_DIM)
            out, last = _rolled(cache_ref[0, rows, :], new_t[rows], T)
            out_ref[0, rows, :n_past - LANES] = out[:, :n_past - LANES]
            out_ref[0, rows, n_past - LANES:] = last


def _layer_norm(z, g, b):
    mu = jnp.mean(z, axis=-1, keepdims=True)
    zc = z - mu
    var = jnp.mean(zc * zc, axis=-1, keepdims=True)
    return zc * lax.rsqrt(var + LN_EPS) * g + b


def _outproj_kernel(x_ref, a_ref, b_ref, wa_ref, wb_ref, g_ref, be_ref, h_ref):
    proj = jnp.dot(a_ref[...].astype(BF16), wa_ref[...], preferred_element_type=F32)
    proj = proj + jnp.dot(b_ref[...].astype(BF16), wb_ref[...], preferred_element_type=F32)
    h_ref[...] = _layer_norm(DN_ALPHA * x_ref[...] + proj, g_ref[...], be_ref[...])


def _outproj_ln(x, a, b, w_a, w_b, g, be, *, tm):
    N, D = x.shape
    assert N % tm == 0
    rows = lambda w: pl.BlockSpec((tm, w), lambda i: (i, 0))
    return pl.pallas_call(
        _outproj_kernel,
        grid=(N // tm,),
        in_specs=[rows(D), rows(a.shape[1]), rows(b.shape[1]), _resident(w_a.shape), _resident(w_b.shape),
                  _resident(g.shape), _resident(be.shape)],
        out_specs=rows(D),
        out_shape=jax.ShapeDtypeStruct((N, D), F32),
        compiler_params=_cparams(("arbitrary",)),
        name="outproj_ln",
    )(x, a, b, w_a, w_b, g, be)


def _merge_outproj_kernel(x_ref, a_ref, o1_ref, l1_ref, o4_ref, l4_ref, o16_ref, l16_ref, e_ref,
                          wa_ref, wb_ref, g_ref, be_ref, h_ref, n4_ref, n16_ref, nl_ref, *, tm):
    m4, m16 = tm // 4, tm // 16
    for r in range(4):
        nl_ref[0, pl.ds(r, m4, stride=4), :] = l4_ref[:, r * LANES:(r + 1) * LANES]
        for g in range(GROUPS_B):
            n4_ref[g, pl.ds(r, m4, stride=4), :] = o4_ref[:, r * QB + g * LANES:r * QB + (g + 1) * LANES].astype(F32)
    for r in range(16):
        nl_ref[1, pl.ds(r, m16, stride=16), :] = l16_ref[:, r * LANES:(r + 1) * LANES]
        for g in range(GROUPS_B):
            n16_ref[g, pl.ds(r, m16, stride=16), :] = (
                o16_ref[:, r * QB + g * LANES:r * QB + (g + 1) * LANES].astype(F32))

    lses = [l1_ref[...], nl_ref[0], nl_ref[1]]
    mx = jnp.maximum(jnp.maximum(lses[0], lses[1]), lses[2])
    ws = [jnp.exp2(l - mx) for l in lses]
    inv = 1.0 / (ws[0] + ws[1] + ws[2])
    e = e_ref[...]

    def widen(w):
        hi = w.astype(BF16)
        lo = (w - hi.astype(F32)).astype(BF16)
        return jnp.dot(hi, e, preferred_element_type=F32) + jnp.dot(lo, e, preferred_element_type=F32)

    ws = [widen(w * inv) for w in ws]
    merged = []
    for g in range(GROUPS_B):
        sl = slice(g * LANES, (g + 1) * LANES)
        bg = ws[0][:, sl] * o1_ref[:, sl].astype(F32) + ws[1][:, sl] * n4_ref[g] + ws[2][:, sl] * n16_ref[g]
        merged.append(bg.astype(BF16))
    proj = jnp.dot(a_ref[...], wa_ref[...], preferred_element_type=F32)
    proj = proj + jnp.dot(jnp.concatenate(merged, axis=1), wb_ref[...], preferred_element_type=F32)
    h_ref[...] = _layer_norm(DN_ALPHA * x_ref[...] + proj, g_ref[...], be_ref[...])


def _merge_outproj_ln(x, a, o1, l1, o4, l4, o16, l16, w_a, w_b, g, be, *, tm):
    N, D = x.shape
    assert N % tm == 0 and tm % 256 == 0
    rows = lambda n, w: pl.BlockSpec((n, w), lambda i: (i, 0))
    expander = jnp.asarray(_lse_expander(), BF16)
    return pl.pallas_call(
        functools.partial(_merge_outproj_kernel, tm=tm),
        grid=(N // tm,),
        in_specs=[rows(tm, D), rows(tm, QA), rows(tm, QB), rows(tm, LANES),
                  rows(tm // 4, 4 * QB), rows(tm // 4, 4 * LANES), rows(tm // 16, 16 * QB), rows(tm // 16, 16 * LANES),
                  _resident(expander.shape), _resident(w_a.shape), _resident(w_b.shape),
                  _resident(g.shape), _resident(be.shape)],
        out_specs=rows(tm, D),
        out_shape=jax.ShapeDtypeStruct((N, D), F32),
        scratch_shapes=[pltpu.VMEM((GROUPS_B, tm, LANES), F32), pltpu.VMEM((GROUPS_B, tm, LANES), F32),
                        pltpu.VMEM((2, tm, LANES), F32)],
        compiler_params=_cparams(("arbitrary",)),
        name="merge_outproj_ln",
    )(x, a, o1, l1, o4, l4, o16, l16, expander, w_a, w_b, g, be)


def _ffn_kernel(*refs, tm, n_chunks, n_splits, n_units, guard_units, T):
    h_hbm, wu_ref, wd_ref, g_ref, be_ref = refs[:5]
    if n_units:
        sample_in = refs[5:12]
        y_ref, b_ref, obk_ref, obv_ref, hb_ref, sem, kbuf, vbuf, ring_sem = refs[-9:]
    else:
        y_ref, hb_ref, sem = refs[5:]
    i = pl.program_id(0)
    j = pl.program_id(1)
    k = pl.program_id(2)

    @pl.when((j == 0) & (k == 0))
    def _():
        copy = pltpu.make_async_copy(h_hbm.at[pl.ds(pl.multiple_of(i * tm, tm), tm), :], y_ref, sem.at[0])
        copy.start()
        copy.wait()
        h = y_ref[...]
        hb_ref[...] = h.astype(BF16)
        y_ref[...] = DN_ALPHA * h

    if n_units:
        qb_ref, kbn_ref, vbn_ref, cbk_hbm, cbv_hbm, mbc_ref, mbn_ref = sample_in
        step = (i * n_chunks + j) * n_splits + k
        width = kbuf.shape[1]
        parts = cbk_hbm.shape[1] // width

        def fetch(unit, slot):
            rows = pl.ds(pl.multiple_of((unit % parts) * width, width), width)
            return (pltpu.make_async_copy(cbk_hbm.at[unit // parts, rows, :], kbuf.at[slot], ring_sem.at[0, slot]),
                    pltpu.make_async_copy(cbv_hbm.at[unit // parts, rows, :], vbuf.at[slot], ring_sem.at[1, slot]))

        @pl.when(step == 0)
        def _():
            for unit in range(min(RING_SLOTS - 1, n_units)):
                for copy in fetch(unit, unit):
                    copy.start()

        @pl.when(step + RING_SLOTS - 1 < n_units)
        def _():
            for copy in fetch(step + RING_SLOTS - 1, (step + RING_SLOTS - 1) % RING_SLOTS):
                copy.start()

        def sample_step():
            slot = step % RING_SLOTS
            for copy in fetch(step, slot):
                copy.wait()
            _sample_dilated_step(qb_ref, kbn_ref, vbn_ref, kbuf.at[pl.ds(slot, 1)], vbuf.at[pl.ds(slot, 1)],
                                 mbc_ref, mbn_ref, b_ref, obk_ref, obv_ref, T)

        if guard_units:
            pl.when(step < n_units)(sample_step)
        else:
            sample_step()

    part = tm // n_splits
    rows = pl.ds(pl.multiple_of(k * part, part), part)
    u = jnp.dot(hb_ref[rows, :], wu_ref[...], preferred_element_type=F32)
    u = jnp.maximum(u, 0.0)
    y_ref[rows, :] += jnp.dot((u * u).astype(BF16), wd_ref[...], preferred_element_type=F32)

    @pl.when((j == n_chunks - 1) & (k == n_splits - 1))
    def _():
        y_ref[...] = _layer_norm(y_ref[...], g_ref[...], be_ref[...])


SAMPLE_UNIT_WIDTH = 4 * HEAD_DIM
RING_SLOTS = 3


def _sample_dilated_masks(T, n_past, width):
    _, _, b_c, b_n = _sample_masks(T, LANES, n_past)
    return [jnp.asarray(np.tile(m, (width // HEAD_DIM, 1))) for m in (b_c, b_n)]


def _sample_dilated(qb, kbn, vbn, cbk, cbv, *, b_lo):
    DB, T, _ = qb.shape
    n_past = cbk.shape[2]
    w = QB // 2
    masks = _sample_dilated_masks(T, n_past, w)
    new = pl.BlockSpec((1, T, w), lambda b, p: (b_lo + b, 0, p))
    cache = pl.BlockSpec((1, w, n_past), lambda b, p: (b_lo + b, p, 0))
    return pl.pallas_call(
        functools.partial(_sample_dilated_step, T=T),
        grid=(DB - b_lo, QB // w),
        in_specs=[new, new, new, cache, cache, _resident(masks[0].shape), _resident(masks[1].shape)],
        out_specs=[pl.BlockSpec((1, T, w), lambda b, p: (b, 0, p)), cache, cache],
        out_shape=[jax.ShapeDtypeStruct((DB - b_lo, T, QB), F32), jax.ShapeDtypeStruct(cbk.shape, F32),
                   jax.ShapeDtypeStruct(cbv.shape, F32)],
        compiler_params=_cparams(("arbitrary", "arbitrary")),
        name="sample_dilated",
    )(qb, kbn, vbn, cbk, cbv, *masks)


def _ffn_ln(h, w_up, w_down, g, be, *, tm, tf, n_splits=1, sample=None):
    N, D = h.shape
    F = w_up.shape[1]
    assert N % tm == 0 and F % tf == 0
    assert tm % (8 * n_splits) == 0
    n_tiles, n_chunks = N // tm, F // tf
    n_steps = n_tiles * n_chunks * n_splits
    in_specs = [pl.BlockSpec(memory_space=pl.ANY),
                pl.BlockSpec((D, tf), lambda i, j, k: (0, j)),
                pl.BlockSpec((tf, D), lambda i, j, k: (j, 0)),
                _resident(g.shape), _resident(be.shape)]
    args = [h, w_up, w_down, g, be]
    out_specs = [pl.BlockSpec((tm, D), lambda i, j, k: (i, 0))]
    out_shape = [jax.ShapeDtypeStruct((N, D), F32)]
    n_units, T, aliases = 0, 0, {}
    scratch = [pltpu.VMEM((tm, D), BF16), pltpu.SemaphoreType.DMA((1,))]
    if sample is not None:
        qb, kbn, vbn, cbk, cbv, n_b, obk, obv = sample
        _, T, _ = qb.shape
        n_past = cbk.shape[2]
        w = SAMPLE_UNIT_WIDTH
        parts = QB // w
        n_units = n_b * parts
        assert 0 < n_units <= n_steps
        masks = _sample_dilated_masks(T, n_past, w)

        def unit(i, j, k):
            u = jnp.minimum((i * n_chunks + j) * n_splits + k, n_units - 1)
            return u // parts, u % parts

        new = pl.BlockSpec((1, T, w), lambda i, j, k: (unit(i, j, k)[0], 0, unit(i, j, k)[1]))
        cache = pl.BlockSpec((1, w, n_past), lambda i, j, k: (unit(i, j, k)[0], unit(i, j, k)[1], 0))
        hbm = pl.BlockSpec(memory_space=pl.ANY)
        in_specs += [new, new, new, hbm, hbm, _resident(masks[0].shape), _resident(masks[1].shape)]
        args += [qb, kbn, vbn, cbk, cbv] + masks
        scratch += [pltpu.VMEM((RING_SLOTS, w, n_past), F32), pltpu.VMEM((RING_SLOTS, w, n_past), F32),
                    pltpu.SemaphoreType.DMA((2, RING_SLOTS))]
        if obk is not None:
            in_specs += [pl.BlockSpec(memory_space=pl.ANY)] * 2
            args += [obk, obv]
            aliases = {len(args) - 2: 2, len(args) - 1: 3}
        out_specs += [new, cache, cache]
        out_shape += [jax.ShapeDtypeStruct((n_b, T, QB), F32), jax.ShapeDtypeStruct(cbk.shape, F32),
                      jax.ShapeDtypeStruct(cbv.shape, F32)]
    outs = pl.pallas_call(
        functools.partial(_ffn_kernel, tm=tm, n_chunks=n_chunks, n_splits=n_splits, n_units=n_units,
                          guard_units=n_units < n_steps, T=T),
        grid=(n_tiles, n_chunks, n_splits),
        in_specs=in_specs,
        out_specs=out_specs,
        out_shape=out_shape,
        scratch_shapes=scratch,
        input_output_aliases=aliases,
        compiler_params=_cparams(("arbitrary",) * 3),
        name="ffn_ln_sample" if sample is not None else "ffn_ln",
    )(*args)
    return outs if sample is not None else outs[0]


def _token_tile(n, cap):
    t = min(n, cap)
    while n % t:
        t //= 2
    return t


def _position_minor(cache):
    _, DB, n, H, Dh = cache.shape
    return jnp.transpose(cache[0], (0, 2, 3, 1)).reshape(DB, H * Dh, n)


def _position_major(x, heads):
    DB, _, n = x.shape
    return jnp.transpose(x.reshape(DB, heads, HEAD_DIM, n), (0, 3, 1, 2))[None]


def kernel(x_prompt, x_sample, cache_a_k, cache_a_v, cache_b_k, cache_b_v,
           w_in, sinks, w_out, ln1_g, ln1_b, w_up, w_down, ln2_g, ln2_b):
    B, L, D = x_prompt.shape
    DB, T, _ = x_sample.shape
    assert w_in.shape[0] == 1, "one layer"
    assert DIL_BRANCHES == ((BLK, 1), (4 * BLK, 4), (16 * BLK, 16))

    w_in_bf = w_in[0].astype(BF16)
    w_qa_bf = _reorder_heads(w_in[0], 1).astype(BF16)
    w_out_a = _reorder_heads(w_out[0], 0).astype(BF16)
    w_out_b = w_out[0][QA:].astype(BF16)
    w_up_bf = w_up[0].astype(BF16)
    w_down_bf = w_down[0].astype(BF16)
    g1, b1, g2, b2 = ln1_g[0][None], ln1_b[0][None], ln2_g[0][None], ln2_b[0][None]

    cos_p, sin_p = _rope_tables(jnp.arange(L))
    (qa, ka, va, qb1, kb1, vb1, qb4, kb4, vb4, qb16, kb16, vb16, pak, pav, pbk, pbv) = _project(
        x_prompt, w_in_bf, w_qa_bf, cos_p, sin_p, tm=_token_tile(L, 256), out_dtype=BF16, spread=True,
        q_scale=SCALE * LOG2E)
    a_p = _band_attention(qa, ka, va, stride=1, inclusive=False, sinks=sinks[0], name="sink_attn")
    o1, l1 = _band_attention(qb1, kb1, vb1, stride=1, inclusive=True, emit_lse=True, name="dilated_r1")
    o4, l4 = _band_attention(qb4, kb4, vb4, stride=4, inclusive=True, emit_lse=True, name="dilated_r4")
    o16, l16 = _band_attention(qb16, kb16, vb16, stride=16, inclusive=True, emit_lse=True, name="dilated_r16")
    N_p = B * L
    flat = lambda z: z.reshape(-1, z.shape[-1])
    h_p = _merge_outproj_ln(flat(x_prompt), flat(a_p), flat(o1), flat(l1), flat(o4), flat(l4), flat(o16), flat(l16),
                            w_out_a, w_out_b, g1, b1, tm=_token_tile(L, 256))

    N_s = DB * T
    cos_s, sin_s = _rope_tables(PAST_LEN + jnp.arange(T))
    cos_s = jnp.tile(cos_s, (DB, 1))
    sin_s = jnp.tile(sin_s, (DB, 1))
    sq = _project(x_sample.reshape(1, N_s, D), w_in_bf, w_qa_bf, cos_s, sin_s, tm=_token_tile(N_s, 256),
                  out_dtype=F32, spread=False, q_scale=SCALE)
    qa_s, kan, van, qb_s, kbn, vbn = [z.reshape(DB, T, z.shape[-1]) for z in sq]
    a_s, sak, sav = _sample_sink_attention(qa_s, kan, van, _position_minor(cache_a_k), _position_minor(cache_a_v),
                                           sinks[0])
    tm_ffn, tf_ffn, splits = _token_tile(N_p, 1024), 512, 2
    cbk, cbv = _position_minor(cache_b_k), _position_minor(cache_b_v)
    n_b = min(DB, (N_p // tm_ffn) * (w_up.shape[2] // tf_ffn) * splits // (QB // SAMPLE_UNIT_WIDTH))
    b_hi, sbk, sbv = _sample_dilated(qb_s, kbn, vbn, cbk, cbv, b_lo=n_b) if n_b < DB else (None, None, None)
    y_p, b_s, sbk, sbv = _ffn_ln(h_p, w_up_bf, w_down_bf, g2, b2, tm=tm_ffn, tf=tf_ffn, n_splits=splits,
                                 sample=(qb_s, kbn, vbn, cbk, cbv, n_b, sbk, sbv))
    if b_hi is not None:
        b_s = jnp.concatenate([b_s, b_hi], axis=0)
    h_s = _outproj_ln(x_sample.reshape(N_s, D), a_s.reshape(N_s, QA), b_s.reshape(N_s, QB),
                      w_out_a, w_out_b, g1, b1, tm=_token_tile(N_s, 512))
    y_s = _ffn_ln(h_s, w_up_bf, w_down_bf, g2, b2, tm=_token_tile(N_s, 1024), tf=512)

    return (y_p.reshape(B, L, D), y_s.reshape(DB, T, D),
            _position_major(pak, N_KV_A), _position_major(pav, N_KV_A),
            _position_major(pbk, N_HEADS_B), _position_major(pbv, N_HEADS_B),
            _position_major(sak, N_KV_A), _position_major(sav, N_KV_A),
            _position_major(sbk, N_HEADS_B), _position_major(sbv, N_HEADS_B))
```

```python
import functools

import numpy as np
import jax
import jax.numpy as jnp
from jax import lax
from jax.experimental import pallas as pl
from jax.experimental.pallas import tpu as pltpu

F32 = jnp.float32
BF16 = jnp.bfloat16

HEAD_DIM = 64
LANES = 128
N_HEADS_A = 16
N_KV_A = 4
GROUP_A = N_HEADS_A // N_KV_A
N_HEADS_B = 16
WIN_A = 128
DIL_BRANCHES = ((128, 1), (512, 4), (2048, 16))
WIN_B = 2048
BLK = 128
ROPE_THETA = 10000.0
LN_EPS = 1e-5
DN_ALPHA = 2.0 ** 0.25
SCALE = HEAD_DIM ** -0.5
LOG2E = 1.4426950408889634
NEG_INF = -1e30
PAST_LEN = 16384
QA = N_HEADS_A * HEAD_DIM
KA = N_KV_A * HEAD_DIM
QB = N_HEADS_B * HEAD_DIM
IN_WIDTH = QA + 2 * KA + 3 * QB
GROUPS_B = QB // LANES
VMEM_LIMIT = 56 * 1024 * 1024
NT_DIMS = (((1,), (1,)), ((), ()))


def _cparams(sem):
    return pltpu.CompilerParams(dimension_semantics=sem, vmem_limit_bytes=VMEM_LIMIT)


def _resident(shape):
    nd = len(shape)
    return pl.BlockSpec(shape, lambda *_: (0,) * nd, pipeline_mode=pl.Buffered(1))


def _qa_head_order():
    return [GROUP_A * (2 * p + half) + j for p in range(N_KV_A // 2) for j in range(GROUP_A) for half in range(2)]


def _reorder_heads(w, axis):
    take = lambda h: lax.slice_in_dim(w, h * HEAD_DIM, (h + 1) * HEAD_DIM, axis=axis)
    return jnp.concatenate([take(h) for h in _qa_head_order()], axis=axis)


def _rope_tables(pos):
    half = HEAD_DIM // 2
    inv = 1.0 / (ROPE_THETA ** (jnp.arange(half, dtype=F32) / half))
    ang = pos.astype(F32)[:, None] * inv[None, :]
    cos = jnp.cos(ang)
    sin = jnp.sin(ang)
    return (jnp.concatenate([cos, cos, cos, cos], -1),
            jnp.concatenate([-sin, sin, -sin, sin], -1))


def _lse_expander():
    e = np.zeros((LANES, QB), np.float32)
    for n in range(GROUPS_B):
        e[n, n * LANES:n * LANES + HEAD_DIM] = 1.0
        e[HEAD_DIM + n, n * LANES + HEAD_DIM:(n + 1) * LANES] = 1.0
    return e


def _proj_kernel(x_ref, w_ref, wqa_ref, cos_ref, sin_ref, *refs, tm, spread, cache_rows_a, q_scale):
    qa_ref, ka_ref, va_ref = refs[:3]
    nat_b = refs[3:6]
    if spread:
        by4, by16, tails, (s1_ref, s2_ref) = refs[6:9], refs[9:12], refs[12:16], refs[16:18]
    xb = x_ref[0].astype(BF16)
    cos = cos_ref[...]
    sin = sin_ref[...]
    lane = lax.broadcasted_iota(jnp.int32, (tm, LANES), 1)
    lo_half = (lane % HEAD_DIM) < (HEAD_DIM // 2)

    def rope(hg):
        swapped = jnp.where(lo_half, pltpu.roll(hg, LANES - HEAD_DIM // 2, 1), pltpu.roll(hg, HEAD_DIM // 2, 1))
        return hg * cos + swapped * sin

    def groups(col0, n, ref=w_ref):
        h = jnp.dot(xb, ref[:, col0:col0 + n * LANES], preferred_element_type=F32)
        return [h[:, g * LANES:(g + 1) * LANES] for g in range(n)]

    def put(ref, g, val):
        ref[0, :, g * LANES:(g + 1) * LANES] = val.astype(ref.dtype)

    def put_spread(which, g, val):
        slab = which * GROUPS_B + g
        n4, n16 = tm // 4, tm // 16
        s1_ref[slab] = val
        for r4 in range(4):
            part = s1_ref[slab, pl.ds(r4, n4, stride=4), :]
            by4[which][0, :, r4 * QB + g * LANES:r4 * QB + (g + 1) * LANES] = part.astype(BF16)
            s2_ref[slab, r4 * n4:(r4 + 1) * n4, :] = part
        for r4 in range(4):
            for a in range(4):
                part = s2_ref[slab, pl.ds(r4 * n4 + a, n16, stride=4), :]
                r16 = 4 * a + r4
                by16[which][0, :, r16 * QB + g * LANES:r16 * QB + (g + 1) * LANES] = part.astype(BF16)

    for c in range(QA // 512):
        for g, hg in enumerate(groups(c * 512, 4, wqa_ref)):
            put(qa_ref, c * 4 + g, rope(hg) * q_scale)

    kva = groups(QA, 4)
    ka_g = [rope(kva[0]), rope(kva[1])]
    va_g = [kva[2], kva[3]]
    for g in range(2):
        put(ka_ref, g, ka_g[g])
        put(va_ref, g, va_g[g])
    if spread:
        for g in range(2):
            tails[0][0, g * LANES:(g + 1) * LANES, :] = ka_g[g][tm - cache_rows_a:, :].T
            tails[1][0, g * LANES:(g + 1) * LANES, :] = va_g[g][tm - cache_rows_a:, :].T

    base = QA + 2 * KA
    for which in range(3):
        for c in range(QB // 512):
            vals = groups(base + QB * which + c * 512, 4)
            if which < 2:
                vals = [rope(v) for v in vals]
            if which == 0:
                vals = [v * q_scale for v in vals]
            for g, v in enumerate(vals):
                put(nat_b[which], c * 4 + g, v)
                if spread:
                    put_spread(which, c * 4 + g, v)
            if spread and which > 0:
                for g, v in enumerate(vals):
                    tails[1 + which][0, (c * 4 + g) * LANES:(c * 4 + g + 1) * LANES, :] = v.T


def _project(x, w_in_bf, w_qa_bf, cos, sin, *, tm, out_dtype, spread, q_scale):
    B, L, D = x.shape
    nt = L // tm
    assert L % tm == 0
    widths = (QA, KA, KA, QB, QB, QB)
    out_shape = [jax.ShapeDtypeStruct((B, L, w), out_dtype) for w in widths]
    out_specs = [pl.BlockSpec((1, tm, w), lambda b, t: (b, t, 0)) for w in widths]
    scratch = []
    rows_a = min(WIN_A, L)
    rows_b = min(WIN_B, L)
    first_cache_tile = (L - rows_b) // tm
    if spread:
        assert tm % 256 == 0 and rows_a <= tm and (L - rows_b) % tm == 0
        for r in (4, 16):
            out_shape += [jax.ShapeDtypeStruct((B, L // r, r * QB), BF16)] * 3
            out_specs += [pl.BlockSpec((1, tm // r, r * QB), lambda b, t: (b, t, 0))] * 3
        out_shape += [jax.ShapeDtypeStruct((B, KA, rows_a), F32)] * 2
        out_shape += [jax.ShapeDtypeStruct((B, QB, rows_b), F32)] * 2
        out_specs += [pl.BlockSpec((1, KA, rows_a), lambda b, t: (b, 0, 0))] * 2
        out_specs += [pl.BlockSpec((1, QB, tm), lambda b, t: (b, 0, jnp.maximum(t - first_cache_tile, 0)))] * 2
        scratch = [pltpu.VMEM((3 * GROUPS_B, tm, LANES), F32)] * 2
    kern = functools.partial(_proj_kernel, tm=tm, spread=spread, cache_rows_a=rows_a, q_scale=q_scale)
    return pl.pallas_call(
        kern,
        grid=(B, nt),
        in_specs=[pl.BlockSpec((1, tm, D), lambda b, t: (b, t, 0)),
                  _resident((D, IN_WIDTH)), _resident((D, QA)),
                  pl.BlockSpec((tm, LANES), lambda b, t: (t, 0)),
                  pl.BlockSpec((tm, LANES), lambda b, t: (t, 0))],
        out_specs=out_specs,
        out_shape=out_shape,
        scratch_shapes=scratch,
        compiler_params=_cparams(("arbitrary", "arbitrary")),
        name="proj_spread" if spread else "proj",
    )(x, w_in_bf, w_qa_bf, cos, sin)


def _band_attn_kernel(*refs, n_kgroups, q_per_k, inclusive, has_sink, emit_lse, n_sub):
    it = iter(refs)
    q_ref, kp_ref, kc_ref, vp_ref, vc_ref = (next(it) for _ in range(5))
    sink_ref = next(it) if has_sink else None
    o_ref = next(it)
    lse_ref = next(it) if emit_lse else None

    mt = pl.program_id(2)
    row = lax.broadcasted_iota(jnp.int32, (2 * BLK, 2 * BLK), 0)
    col = lax.broadcasted_iota(jnp.int32, (2 * BLK, 2 * BLK), 1)
    dist = (row % BLK) + BLK - col
    in_band = (dist >= 0) & ((dist <= BLK) if inclusive else (dist < BLK))
    first_valid = in_band & ((col >= BLK) | (mt > 0))
    lane = lax.broadcasted_iota(jnp.int32, (BLK, LANES), 1)
    lane_lo = lane < HEAD_DIM
    upper_rows = lax.broadcasted_iota(jnp.int32, (2 * BLK, 1), 0) >= BLK
    ones = jnp.ones((2 * BLK, LANES), BF16)
    zero_q = jnp.zeros((BLK, LANES), BF16)
    lse_c = [jnp.zeros((BLK, LANES), F32)] * n_sub

    for n in range(n_kgroups):
        ksl = slice(n * LANES, (n + 1) * LANES)
        kblk = [kp_ref[0, :, ksl]] + [kc_ref[0, i * BLK:(i + 1) * BLK, ksl] for i in range(n_sub)]
        vblk = [vp_ref[0, :, ksl]] + [vc_ref[0, i * BLK:(i + 1) * BLK, ksl] for i in range(n_sub)]
        for sub in range(n_sub):
            rows = slice(sub * BLK, (sub + 1) * BLK)
            valid = first_valid if sub == 0 else in_band
            k2 = jnp.concatenate([kblk[sub], kblk[sub + 1]], axis=0)
            v2e = jnp.concatenate([jnp.concatenate([vblk[sub], vblk[sub + 1]], axis=0), ones], axis=1)
            for j in range(q_per_k):
                g = n * q_per_k + j
                gsl = slice(g * LANES, (g + 1) * LANES)
                qg = q_ref[0, rows, gsl]
                qs = jnp.concatenate([jnp.where(lane_lo, qg, zero_q), jnp.where(lane_lo, zero_q, qg)], axis=0)
                s = lax.dot_general(qs, k2, NT_DIMS, preferred_element_type=F32)
                s = jnp.where(valid, s, NEG_INF)
                m = jnp.max(s, axis=1, keepdims=True)
                if has_sink:
                    lo_head = 2 * GROUP_A * n + j
                    sink = jnp.where(upper_rows, sink_ref[lo_head + GROUP_A], sink_ref[lo_head]) * LOG2E
                    m = jnp.maximum(m, sink)
                p = jnp.exp2(s - m)
                r = jnp.dot(p.astype(BF16), v2e, preferred_element_type=F32)
                l = r[:, LANES:]
                if has_sink:
                    l = l + jnp.exp2(sink - m)
                on = r[:, :LANES] / l
                o_ref[0, rows, gsl] = jnp.where(lane_lo, on[:BLK], on[BLK:]).astype(o_ref.dtype)
                if emit_lse:
                    lse_rows = m + jnp.log2(l)
                    lse = jnp.where(lane_lo, lse_rows[:BLK], lse_rows[BLK:])
                    lse_c[sub] = jnp.where((lane == g) | (lane == HEAD_DIM + g), lse, lse_c[sub])
    if emit_lse:
        for sub in range(n_sub):
            lse_ref[0, sub * BLK:(sub + 1) * BLK, :] = lse_c[sub]


def _band_attention(q, k, v, *, stride, inclusive, sinks=None, emit_lse=False, name="band_attn"):
    B, M, _ = q.shape
    Wq, Wk = q.shape[-1] // stride, k.shape[-1] // stride
    assert M % BLK == 0
    n_sub = max(d for d in (1, 2, 4) if (M // BLK) % d == 0)
    tq = n_sub * BLK
    cur = lambda w: pl.BlockSpec((1, tq, w), lambda b, r, m: (b, m, r))
    prv = lambda w: pl.BlockSpec((1, BLK, w), lambda b, r, m: (b, jnp.maximum(m * n_sub - 1, 0), r))
    args = [q, k, k, v, v]
    in_specs = [cur(Wq), prv(Wk), cur(Wk), prv(Wk), cur(Wk)]
    if sinks is not None:
        args.append(sinks)
        in_specs.append(pl.BlockSpec(memory_space=pltpu.SMEM))
    out_shape = [jax.ShapeDtypeStruct(q.shape, BF16)]
    out_specs = [cur(Wq)]
    if emit_lse:
        out_shape.append(jax.ShapeDtypeStruct((B, M, stride * LANES), F32))
        out_specs.append(cur(LANES))
    kern = functools.partial(_band_attn_kernel, n_kgroups=Wk // LANES, q_per_k=Wq // Wk, inclusive=inclusive,
                             has_sink=sinks is not None, emit_lse=emit_lse, n_sub=n_sub)
    outs = pl.pallas_call(
        kern,
        grid=(B, stride, M // tq),
        in_specs=in_specs,
        out_specs=out_specs,
        out_shape=out_shape,
        compiler_params=_cparams(("arbitrary",) * 3),
        name=name,
    )(*args)
    return outs if emit_lse else outs[0]


def _sample_masks(T, n_past_a, n_past_b):
    t = np.arange(T)[:, None]
    dist_c = n_past_a + t - np.arange(n_past_a)[None, :]
    dist_n = t - np.arange(LANES)[None, :]
    a_c = ((dist_c >= 0) & (dist_c < WIN_A)).astype(np.float32)
    a_n = ((dist_n >= 0) & (dist_n < WIN_A)).astype(np.float32)
    dist_c = n_past_b + t - np.arange(n_past_b)[None, :]
    b_c = np.zeros((T, n_past_b), np.float32)
    b_n = np.zeros((T, LANES), np.float32)
    for window, r in DIL_BRANCHES:
        b_c += ((dist_c >= 0) & (dist_c % r == 0) & (dist_c // r <= window // r)).astype(np.float32)
        b_n += ((dist_n >= 0) & (dist_n % r == 0) & (dist_n // r <= window // r)).astype(np.float32)
    return a_c, a_n, b_c, b_n


def _pad_rows(x, top):
    z = jnp.zeros((LANES - x.shape[0], x.shape[1]), F32)
    return jnp.concatenate([x, z] if top else [z, x], axis=0)


def _rolled(cache, new_t, T):
    n = cache.shape[1]
    out = pltpu.roll(cache, n - T, 1)
    keep = lax.broadcasted_iota(jnp.int32, (cache.shape[0], LANES), 1) < LANES - T
    return out, jnp.where(keep, out[:, n - LANES:], new_t)


def _softmax_parts(s_c, s_n, m_c, m_n, sink=None):
    s_c = jnp.where(m_c > 0, s_c, NEG_INF)
    s_n = jnp.where(m_n > 0, s_n, NEG_INF)
    m = jnp.maximum(jnp.max(s_c, axis=1, keepdims=True), jnp.max(s_n, axis=1, keepdims=True))
    if sink is not None:
        m = jnp.maximum(m, sink)
    p_c = m_c * jnp.exp(s_c - m)
    p_n = m_n * jnp.exp(s_n - m)
    l = jnp.sum(p_c, axis=1, keepdims=True) + jnp.sum(p_n, axis=1, keepdims=True)
    if sink is not None:
        l = l + jnp.exp(sink - m)
    return p_c, p_n, l


def _sample_sink_kernel(*refs, T):
    for bi in range(refs[0].shape[0]):
        _sample_sink_one(bi, *refs, T)


def _sample_sink_one(bi, qa_ref, kan_ref, van_ref, cak_ref, cav_ref, sink_ref, mac_ref, man_ref,
                     a_ref, oak_ref, oav_ref, T):
    n_past = cak_ref.shape[2]
    for cache_ref, new_ref, out_ref in ((cak_ref, kan_ref, oak_ref), (cav_ref, van_ref, oav_ref)):
        out, last = _rolled(cache_ref[bi], _pad_rows(new_ref[bi], top=False).T, T)
        if n_past > LANES:
            out_ref[bi, :, :n_past - LANES] = out[:, :n_past - LANES]
        out_ref[bi, :, n_past - LANES:] = last

    lo = lax.broadcasted_iota(jnp.int32, (T, LANES), 1) < HEAD_DIM
    for p in range(N_KV_A // 2):
        ksl = slice(p * LANES, (p + 1) * LANES)
        blocks = []
        for j in range(GROUP_A):
            qg = qa_ref[bi, :, (p * GROUP_A + j) * LANES:(p * GROUP_A + j + 1) * LANES]
            blocks += [jnp.where(lo, qg, 0.0), jnp.where(lo, 0.0, qg)]
        qs = jnp.concatenate(blocks, axis=0).astype(BF16)
        s_c = jnp.dot(qs, cak_ref[bi, ksl, :].astype(BF16), preferred_element_type=F32)
        s_n = lax.dot_general(qs, _pad_rows(kan_ref[bi, :, ksl], top=True).astype(BF16), NT_DIMS,
                              preferred_element_type=F32)
        p_c, p_n, l = _softmax_parts(s_c, s_n, mac_ref[...], man_ref[...], sink_ref[p][:, 0:1])
        r = lax.dot_general(p_c.astype(BF16), cav_ref[bi, ksl, :].astype(BF16), NT_DIMS, preferred_element_type=F32)
        r = r + jnp.dot(p_n.astype(BF16), _pad_rows(van_ref[bi, :, ksl], top=True).astype(BF16),
                        preferred_element_type=F32)
        r = r / l
        for j in range(GROUP_A):
            g = p * GROUP_A + j
            a_ref[bi, :, g * LANES:(g + 1) * LANES] = jnp.where(lo, r[(2 * j) * T:(2 * j + 1) * T],
                                                               r[(2 * j + 1) * T:(2 * j + 2) * T])


def _sample_sink_attention(qa, kan, van, cak, cav, sinks):
    DB, T, _ = qa.shape
    n_past = cak.shape[2]
    assert T % 8 == 0 and T <= LANES and n_past % LANES == 0
    a_c, a_n, _, _ = _sample_masks(T, n_past, LANES)
    masks = [jnp.asarray(np.tile(m, (2 * GROUP_A, 1))) for m in (a_c, a_n)]
    heads = np.asarray([[GROUP_A * (2 * p + half) + j for j in range(GROUP_A) for half in range(2) for _ in range(T)]
                        for p in range(N_KV_A // 2)], np.int32)
    sink_rows = jnp.broadcast_to(sinks[heads][:, :, None], heads.shape + (LANES,)).astype(F32)
    nb = max(d for d in (1, 2, 4, 8) if DB % d == 0)
    per_b = lambda r, w: pl.BlockSpec((nb, r, w), lambda b: (b, 0, 0))
    return pl.pallas_call(
        functools.partial(_sample_sink_kernel, T=T),
        grid=(DB // nb,),
        in_specs=[per_b(T, QA), per_b(T, KA), per_b(T, KA), per_b(KA, n_past), per_b(KA, n_past),
                  _resident(sink_rows.shape), _resident(masks[0].shape), _resident(masks[1].shape)],
        out_specs=[per_b(T, QA), per_b(KA, n_past), per_b(KA, n_past)],
        out_shape=[jax.ShapeDtypeStruct((DB, T, QA), F32), jax.ShapeDtypeStruct(cak.shape, F32),
                   jax.ShapeDtypeStruct(cav.shape, F32)],
        compiler_params=_cparams(("arbitrary",)),
        name="sample_sink_attn",
    )(qa, kan, van, cak, cav, sink_rows, *masks)


def _sample_dilated_step(qb_ref, kbn_ref, vbn_ref, cbk_ref, cbv_ref, mbc_ref, mbn_ref, b_ref, obk_ref, obv_ref, T):
    width = qb_ref.shape[2]
    n_past = cbk_ref.shape[2]
    n_heads = width // HEAD_DIM
    rows_b = n_heads * T
    row_head = lax.broadcasted_iota(jnp.int32, (rows_b, width), 0) // T
    lane_head = lax.broadcasted_iota(jnp.int32, (rows_b, width), 1) // HEAD_DIM
    diag = row_head == lane_head
    qblk = jnp.where(diag, jnp.concatenate([qb_ref[0]] * n_heads, axis=0), 0.0).astype(BF16)
    s_c = jnp.dot(qblk, cbk_ref[0].astype(BF16), preferred_element_type=F32)
    s_n = lax.dot_general(qblk, _pad_rows(kbn_ref[0], top=True).astype(BF16), NT_DIMS, preferred_element_type=F32)
    p_c, p_n, l = _softmax_parts(s_c, s_n, mbc_ref[...], mbn_ref[...])
    r = lax.dot_general(p_c.astype(BF16), cbv_ref[0].astype(BF16), NT_DIMS, preferred_element_type=F32)
    r = r + jnp.dot(p_n.astype(BF16), _pad_rows(vbn_ref[0], top=True).astype(BF16), preferred_element_type=F32)
    r = jnp.where(diag, r / l, 0.0)
    out = r[0:T]
    for h in range(1, n_heads):
        out = out + r[h * T:(h + 1) * T]
    b_ref[0] = out

    for cache_ref, new_ref, out_ref in ((cbk_ref, kbn_ref, obk_ref), (cbv_ref, vbn_ref, obv_ref)):
        new_t = _pad_rows(new_ref[0], top=False).T
        for h in range(n_heads):
            rows = slice(h * HEAD_DIM, (h + 1) * HEAD_DIM)
            out, last = _rolled(cache_ref[0, rows, :], new_t[rows], T)
            out_ref[0, rows, :n_past - LANES] = out[:, :n_past - LANES]
            out_ref[0, rows, n_past - LANES:] = last


def _layer_norm(z, g, b):
    mu = jnp.mean(z, axis=-1, keepdims=True)
    zc = z - mu
    var = jnp.mean(zc * zc, axis=-1, keepdims=True)
    return zc * lax.rsqrt(var + LN_EPS) * g + b


def _outproj_kernel(x_ref, a_ref, b_ref, wa_ref, wb_ref, g_ref, be_ref, h_ref):
    proj = jnp.dot(a_ref[...].astype(BF16), wa_ref[...], preferred_element_type=F32)
    proj = proj + jnp.dot(b_ref[...].astype(BF16), wb_ref[...], preferred_element_type=F32)
    h_ref[...] = _layer_norm(DN_ALPHA * x_ref[...] + proj, g_ref[...], be_ref[...])


def _outproj_ln(x, a, b, w_a, w_b, g, be, *, tm):
    N, D = x.shape
    assert N % tm == 0
    rows = lambda w: pl.BlockSpec((tm, w), lambda i: (i, 0))
    return pl.pallas_call(
        _outproj_kernel,
        grid=(N // tm,),
        in_specs=[rows(D), rows(a.shape[1]), rows(b.shape[1]), _resident(w_a.shape), _resident(w_b.shape),
                  _resident(g.shape), _resident(be.shape)],
        out_specs=rows(D),
        out_shape=jax.ShapeDtypeStruct((N, D), F32),
        compiler_params=_cparams(("arbitrary",)),
        name="outproj_ln",
    )(x, a, b, w_a, w_b, g, be)


def _merge_outproj_kernel(x_ref, a_ref, o1_ref, l1_ref, o4_ref, l4_ref, o16_ref, l16_ref, e_ref,
                          wa_ref, wb_ref, g_ref, be_ref, h_ref, n4_ref, n16_ref, nl_ref, *, tm):
    m4, m16 = tm // 4, tm // 16
    for r in range(4):
        nl_ref[0, pl.ds(r, m4, stride=4), :] = l4_ref[:, r * LANES:(r + 1) * LANES]
        for g in range(GROUPS_B):
            n4_ref[g, pl.ds(r, m4, stride=4), :] = o4_ref[:, r * QB + g * LANES:r * QB + (g + 1) * LANES].astype(F32)
    for r in range(16):
        nl_ref[1, pl.ds(r, m16, stride=16), :] = l16_ref[:, r * LANES:(r + 1) * LANES]
        for g in range(GROUPS_B):
            n16_ref[g, pl.ds(r, m16, stride=16), :] = (
                o16_ref[:, r * QB + g * LANES:r * QB + (g + 1) * LANES].astype(F32))

    lses = [l1_ref[...], nl_ref[0], nl_ref[1]]
    mx = jnp.maximum(jnp.maximum(lses[0], lses[1]), lses[2])
    ws = [jnp.exp2(l - mx) for l in lses]
    inv = 1.0 / (ws[0] + ws[1] + ws[2])
    e = e_ref[...]

    def widen(w):
        hi = w.astype(BF16)
        lo = (w - hi.astype(F32)).astype(BF16)
        return jnp.dot(hi, e, preferred_element_type=F32) + jnp.dot(lo, e, preferred_element_type=F32)

    ws = [widen(w * inv) for w in ws]
    merged = []
    for g in range(GROUPS_B):
        sl = slice(g * LANES, (g + 1) * LANES)
        bg = ws[0][:, sl] * o1_ref[:, sl].astype(F32) + ws[1][:, sl] * n4_ref[g] + ws[2][:, sl] * n16_ref[g]
        merged.append(bg.astype(BF16))
    proj = jnp.dot(a_ref[...], wa_ref[...], preferred_element_type=F32)
    proj = proj + jnp.dot(jnp.concatenate(merged, axis=1), wb_ref[...], preferred_element_type=F32)
    h_ref[...] = _layer_norm(DN_ALPHA * x_ref[...] + proj, g_ref[...], be_ref[...])


def _merge_outproj_ln(x, a, o1, l1, o4, l4, o16, l16, w_a, w_b, g, be, *, tm):
    N, D = x.shape
    assert N % tm == 0 and tm % 256 == 0
    rows = lambda n, w: pl.BlockSpec((n, w), lambda i: (i, 0))
    expander = jnp.asarray(_lse_expander(), BF16)
    return pl.pallas_call(
        functools.partial(_merge_outproj_kernel, tm=tm),
        grid=(N // tm,),
        in_specs=[rows(tm, D), rows(tm, QA), rows(tm, QB), rows(tm, LANES),
                  rows(tm // 4, 4 * QB), rows(tm // 4, 4 * LANES), rows(tm // 16, 16 * QB), rows(tm // 16, 16 * LANES),
                  _resident(expander.shape), _resident(w_a.shape), _resident(w_b.shape),
                  _resident(g.shape), _resident(be.shape)],
        out_specs=rows(tm, D),
        out_shape=jax.ShapeDtypeStruct((N, D), F32),
        scratch_shapes=[pltpu.VMEM((GROUPS_B, tm, LANES), F32), pltpu.VMEM((GROUPS_B, tm, LANES), F32),
                        pltpu.VMEM((2, tm, LANES), F32)],
        compiler_params=_cparams(("arbitrary",)),
        name="merge_outproj_ln",
    )(x, a, o1, l1, o4, l4, o16, l16, expander, w_a, w_b, g, be)


def _ffn_kernel(*refs, tm, n_chunks, n_splits, n_units, guard_units, T):
    h_hbm, wu_ref, wd_ref, g_ref, be_ref = refs[:5]
    if n_units:
        sample_in = refs[5:12]
        y_ref, b_ref, obk_ref, obv_ref, hb_ref, sem, kbuf, vbuf, ring_sem = refs[-9:]
    else:
        y_ref, hb_ref, sem = refs[5:]
    i = pl.program_id(0)
    j = pl.program_id(1)
    k = pl.program_id(2)

    @pl.when((j == 0) & (k == 0))
    def _():
        copy = pltpu.make_async_copy(h_hbm.at[pl.ds(pl.multiple_of(i * tm, tm), tm), :], y_ref, sem.at[0])
        copy.start()
        copy.wait()
        h = y_ref[...]
        hb_ref[...] = h.astype(BF16)
        y_ref[...] = DN_ALPHA * h

    if n_units:
        qb_ref, kbn_ref, vbn_ref, cbk_hbm, cbv_hbm, mbc_ref, mbn_ref = sample_in
        step = (i * n_chunks + j) * n_splits + k
        width = kbuf.shape[1]
        parts = cbk_hbm.shape[1] // width

        def fetch(unit, slot):
            rows = pl.ds(pl.multiple_of((unit % parts) * width, width), width)
            return (pltpu.make_async_copy(cbk_hbm.at[unit // parts, rows, :], kbuf.at[slot], ring_sem.at[0, slot]),
                    pltpu.make_async_copy(cbv_hbm.at[unit // parts, rows, :], vbuf.at[slot], ring_sem.at[1, slot]))

        @pl.when(step == 0)
        def _():
            for unit in range(min(RING_SLOTS - 1, n_units)):
                for copy in fetch(unit, unit):
                    copy.start(priority=RING_DMA_PRIORITY)

        @pl.when(step + RING_SLOTS - 1 < n_units)
        def _():
            for copy in fetch(step + RING_SLOTS - 1, (step + RING_SLOTS - 1) % RING_SLOTS):
                copy.start(priority=RING_DMA_PRIORITY)

        def sample_step():
            slot = step % RING_SLOTS
            for copy in fetch(step, slot):
                copy.wait()
            _sample_dilated_step(qb_ref, kbn_ref, vbn_ref, kbuf.at[pl.ds(slot, 1)], vbuf.at[pl.ds(slot, 1)],
                                 mbc_ref, mbn_ref, b_ref, obk_ref, obv_ref, T)

        if guard_units:
            pl.when(step < n_units)(sample_step)
        else:
            sample_step()

    part = tm // n_splits
    rows = pl.ds(pl.multiple_of(k * part, part), part)
    u = jnp.dot(hb_ref[rows, :], wu_ref[...], preferred_element_type=F32)
    u = jnp.maximum(u, 0.0)
    y_ref[rows, :] += jnp.dot((u * u).astype(BF16), wd_ref[...], preferred_element_type=F32)

    @pl.when((j == n_chunks - 1) & (k == n_splits - 1))
    def _():
        y_ref[...] = _layer_norm(y_ref[...], g_ref[...], be_ref[...])


SAMPLE_UNIT_WIDTH = 4 * HEAD_DIM
RING_SLOTS = 3
RING_DMA_PRIORITY = 1


def _sample_dilated_masks(T, n_past, width):
    _, _, b_c, b_n = _sample_masks(T, LANES, n_past)
    return [jnp.asarray(np.tile(m, (width // HEAD_DIM, 1))) for m in (b_c, b_n)]


def _sample_dilated(qb, kbn, vbn, cbk, cbv, *, b_lo):
    DB, T, _ = qb.shape
    n_past = cbk.shape[2]
    w = QB // 2
    masks = _sample_dilated_masks(T, n_past, w)
    new = pl.BlockSpec((1, T, w), lambda b, p: (b_lo + b, 0, p))
    cache = pl.BlockSpec((1, w, n_past), lambda b, p: (b_lo + b, p, 0))
    return pl.pallas_call(
        functools.partial(_sample_dilated_step, T=T),
        grid=(DB - b_lo, QB // w),
        in_specs=[new, new, new, cache, cache, _resident(masks[0].shape), _resident(masks[1].shape)],
        out_specs=[pl.BlockSpec((1, T, w), lambda b, p: (b, 0, p)), cache, cache],
        out_shape=[jax.ShapeDtypeStruct((DB - b_lo, T, QB), F32), jax.ShapeDtypeStruct(cbk.shape, F32),
                   jax.ShapeDtypeStruct(cbv.shape, F32)],
        compiler_params=_cparams(("arbitrary", "arbitrary")),
        name="sample_dilated",
    )(qb, kbn, vbn, cbk, cbv, *masks)


def _ffn_ln(h, w_up, w_down, g, be, *, tm, tf, n_splits=1, sample=None):
    N, D = h.shape
    F = w_up.shape[1]
    assert N % tm == 0 and F % tf == 0
    assert tm % (8 * n_splits) == 0
    n_tiles, n_chunks = N // tm, F // tf
    n_steps = n_tiles * n_chunks * n_splits
    in_specs = [pl.BlockSpec(memory_space=pl.ANY),
                pl.BlockSpec((D, tf), lambda i, j, k: (0, j)),
                pl.BlockSpec((tf, D), lambda i, j, k: (j, 0)),
                _resident(g.shape), _resident(be.shape)]
    args = [h, w_up, w_down, g, be]
    out_specs = [pl.BlockSpec((tm, D), lambda i, j, k: (i, 0))]
    out_shape = [jax.ShapeDtypeStruct((N, D), F32)]
    n_units, T, aliases = 0, 0, {}
    scratch = [pltpu.VMEM((tm, D), BF16), pltpu.SemaphoreType.DMA((1,))]
    if sample is not None:
        qb, kbn, vbn, cbk, cbv, n_b, obk, obv = sample
        _, T, _ = qb.shape
        n_past = cbk.shape[2]
        w = SAMPLE_UNIT_WIDTH
        parts = QB // w
        n_units = n_b * parts
        assert 0 < n_units <= n_steps
        masks = _sample_dilated_masks(T, n_past, w)

        def unit(i, j, k):
            u = jnp.minimum((i * n_chunks + j) * n_splits + k, n_units - 1)
            return u // parts, u % parts

        new = pl.BlockSpec((1, T, w), lambda i, j, k: (unit(i, j, k)[0], 0, unit(i, j, k)[1]))
        cache = pl.BlockSpec((1, w, n_past), lambda i, j, k: (unit(i, j, k)[0], unit(i, j, k)[1], 0))
        hbm = pl.BlockSpec(memory_space=pl.ANY)
        in_specs += [new, new, new, hbm, hbm, _resident(masks[0].shape), _resident(masks[1].shape)]
        args += [qb, kbn, vbn, cbk, cbv] + masks
        scratch += [pltpu.VMEM((RING_SLOTS, w, n_past), F32), pltpu.VMEM((RING_SLOTS, w, n_past), F32),
                    pltpu.SemaphoreType.DMA((2, RING_SLOTS))]
        if obk is not None:
            in_specs += [pl.BlockSpec(memory_space=pl.ANY)] * 2
            args += [obk, obv]
            aliases = {len(args) - 2: 2, len(args) - 1: 3}
        out_specs += [new, cache, cache]
        out_shape += [jax.ShapeDtypeStruct((n_b, T, QB), F32), jax.ShapeDtypeStruct(cbk.shape, F32),
                      jax.ShapeDtypeStruct(cbv.shape, F32)]
    outs = pl.pallas_call(
        functools.partial(_ffn_kernel, tm=tm, n_chunks=n_chunks, n_splits=n_splits, n_units=n_units,
                          guard_units=n_units < n_steps, T=T),
        grid=(n_tiles, n_chunks, n_splits),
        in_specs=in_specs,
        out_specs=out_specs,
        out_shape=out_shape,
        scratch_shapes=scratch,
        input_output_aliases=aliases,
        compiler_params=_cparams(("arbitrary",) * 3),
        name="ffn_ln_sample" if sample is not None else "ffn_ln",
    )(*args)
    return outs if sample is not None else outs[0]


def _token_tile(n, cap):
    t = min(n, cap)
    while n % t:
        t //= 2
    return t


def _position_minor(cache):
    _, DB, n, H, Dh = cache.shape
    return jnp.transpose(cache[0], (0, 2, 3, 1)).reshape(DB, H * Dh, n)


def _position_major(x, heads):
    DB, _, n = x.shape
    return jnp.transpose(x.reshape(DB, heads, HEAD_DIM, n), (0, 3, 1, 2))[None]


def kernel(x_prompt, x_sample, cache_a_k, cache_a_v, cache_b_k, cache_b_v,
           w_in, sinks, w_out, ln1_g, ln1_b, w_up, w_down, ln2_g, ln2_b):
    B, L, D = x_prompt.shape
    DB, T, _ = x_sample.shape
    assert w_in.shape[0] == 1, "one layer"
    assert DIL_BRANCHES == ((BLK, 1), (4 * BLK, 4), (16 * BLK, 16))

    w_in_bf = w_in[0].astype(BF16)
    w_qa_bf = _reorder_heads(w_in[0], 1).astype(BF16)
    w_out_a = _reorder_heads(w_out[0], 0).astype(BF16)
    w_out_b = w_out[0][QA:].astype(BF16)
    w_up_bf = w_up[0].astype(BF16)
    w_down_bf = w_down[0].astype(BF16)
    g1, b1, g2, b2 = ln1_g[0][None], ln1_b[0][None], ln2_g[0][None], ln2_b[0][None]

    cos_p, sin_p = _rope_tables(jnp.arange(L))
    (qa, ka, va, qb1, kb1, vb1, qb4, kb4, vb4, qb16, kb16, vb16, pak, pav, pbk, pbv) = _project(
        x_prompt, w_in_bf, w_qa_bf, cos_p, sin_p, tm=_token_tile(L, 256), out_dtype=BF16, spread=True,
        q_scale=SCALE * LOG2E)
    a_p = _band_attention(qa, ka, va, stride=1, inclusive=False, sinks=sinks[0], name="sink_attn")
    o1, l1 = _band_attention(qb1, kb1, vb1, stride=1, inclusive=True, emit_lse=True, name="dilated_r1")
    o4, l4 = _band_attention(qb4, kb4, vb4, stride=4, inclusive=True, emit_lse=True, name="dilated_r4")
    o16, l16 = _band_attention(qb16, kb16, vb16, stride=16, inclusive=True, emit_lse=True, name="dilated_r16")
    N_p = B * L
    flat = lambda z: z.reshape(-1, z.shape[-1])
    h_p = _merge_outproj_ln(flat(x_prompt), flat(a_p), flat(o1), flat(l1), flat(o4), flat(l4), flat(o16), flat(l16),
                            w_out_a, w_out_b, g1, b1, tm=_token_tile(L, 256))

    N_s = DB * T
    cos_s, sin_s = _rope_tables(PAST_LEN + jnp.arange(T))
    cos_s = jnp.tile(cos_s, (DB, 1))
    sin_s = jnp.tile(sin_s, (DB, 1))
    sq = _project(x_sample.reshape(1, N_s, D), w_in_bf, w_qa_bf, cos_s, sin_s, tm=_token_tile(N_s, 256),
                  out_dtype=F32, spread=False, q_scale=SCALE)
    qa_s, kan, van, qb_s, kbn, vbn = [z.reshape(DB, T, z.shape[-1]) for z in sq]
    a_s, sak, sav = _sample_sink_attention(qa_s, kan, van, _position_minor(cache_a_k), _position_minor(cache_a_v),
                                           sinks[0])
    tm_ffn, tf_ffn, splits = _token_tile(N_p, 1024), 512, 2
    cbk, cbv = _position_minor(cache_b_k), _position_minor(cache_b_v)
    n_b = min(DB, (N_p // tm_ffn) * (w_up.shape[2] // tf_ffn) * splits // (QB // SAMPLE_UNIT_WIDTH))
    b_hi, sbk, sbv = _sample_dilated(qb_s, kbn, vbn, cbk, cbv, b_lo=n_b) if n_b < DB else (None, None, None)
    y_p, b_s, sbk, sbv = _ffn_ln(h_p, w_up_bf, w_down_bf, g2, b2, tm=tm_ffn, tf=tf_ffn, n_splits=splits,
                                 sample=(qb_s, kbn, vbn, cbk, cbv, n_b, sbk, sbv))
    if b_hi is not None:
        b_s = jnp.concatenate([b_s, b_hi], axis=0)
    h_s = _outproj_ln(x_sample.reshape(N_s, D), a_s.reshape(N_s, QA), b_s.reshape(N_s, QB),
                      w_out_a, w_out_b, g1, b1, tm=_token_tile(N_s, 512))
    y_s = _ffn_ln(h_s, w_up_bf, w_down_bf, g2, b2, tm=_token_tile(N_s, 1024), tf=512)

    return (y_p.reshape(B, L, D), y_s.reshape(DB, T, D),
            _position_major(pak, N_KV_A), _position_major(pav, N_KV_A),
            _position_major(pbk, N_HEADS_B), _position_major(pbv, N_HEADS_B),
            _position_major(sak, N_KV_A), _position_major(sav, N_KV_A),
            _position_major(sbk, N_HEADS_B), _position_major(sbv, N_HEADS_B))
```
